```python
import math
import jax, jax.numpy as jnp
from jax import lax
import numpy as np

D_MODEL = 2048
BATCH = 8
SEQ = 2048
DEPTH = 2

N_MIXERS = 2
N_MLA_LAYERS = (DEPTH + 1) // 2
N_GDN_LAYERS = DEPTH // 2
RMS_EPS = 1e-6

MLA_HEADS = D_MODEL // 128
Q_LORA = D_MODEL // 4
KV_LORA = D_MODEL // 4
QK_NOPE = 128
QK_ROPE = 64
V_HEAD = 128
ROPE_THETA = 10000.0
Q_BLOCK = 128
MLA_IN = Q_LORA + KV_LORA + QK_ROPE

GDN_HEADS = D_MODEL // 128
GDN_DK = 128
GDN_DV = 128
CONV_WIDTH = 4
CHUNK = 64
QKV_DIM = GDN_HEADS * (2 * GDN_DK + GDN_DV)
GDN_VAL = GDN_HEADS * GDN_DV
GDN_IN = QKV_DIM + GDN_VAL + 2 * GDN_HEADS

D_FF = 7 * D_MODEL // 2
N_EXPERTS = 8
TOP_K = 2
D_FF_EXPERT = 7 * D_MODEL // 2
MOE_BLOCK = 256

kernel_name = 'mla_gdn_interleaved_moe_trunk'


def rms_norm(x, w):
    xf = x.astype(jnp.float32)
    y = xf * lax.rsqrt(jnp.mean(xf * xf, axis=-1, keepdims=True) + RMS_EPS)
    return (y * w.astype(jnp.float32)).astype(x.dtype)


def l2norm(x):
    xf = x.astype(jnp.float32)
    return xf * lax.rsqrt(jnp.sum(xf * xf, axis=-1, keepdims=True) + RMS_EPS)


def swiglu(x, w_gate, w_up, w_down):
    return (jax.nn.silu(x @ w_gate) * (x @ w_up)) @ w_down


def rope_angles(positions, dim):
    inv_freq = ROPE_THETA ** (-jnp.arange(0, dim, 2, dtype=jnp.float32) / dim)
    ang = positions.astype(jnp.float32)[..., None] * inv_freq
    return jnp.cos(ang), jnp.sin(ang)


def apply_rope(x, cos, sin):
    x1, x2 = jnp.split(x, 2, axis=-1)
    return jnp.concatenate([x1 * cos - x2 * sin, x2 * cos + x1 * sin], axis=-1).astype(x.dtype)


def mla_attention(h, positions, w_in, q_norm, w_qb, kv_norm, w_kvb, w_o):
    B, S, _ = h.shape
    proj = h @ w_in
    c_q, c_kv, k_rope = jnp.split(proj, [Q_LORA, Q_LORA + KV_LORA], axis=-1)
    q = (rms_norm(c_q, q_norm) @ w_qb).reshape(B, S, MLA_HEADS, QK_NOPE + QK_ROPE)
    kv = (rms_norm(c_kv, kv_norm) @ w_kvb).reshape(B, S, MLA_HEADS, QK_NOPE + V_HEAD)
    q_nope, q_pe = jnp.split(q, [QK_NOPE], axis=-1)
    k_nope, v = jnp.split(kv, [QK_NOPE], axis=-1)
    cos, sin = rope_angles(positions, QK_ROPE)
    q_pe = apply_rope(q_pe, cos[:, :, None, :], sin[:, :, None, :])
    k_pe = apply_rope(k_rope, cos, sin)
    q_nope, q_pe, k_nope, v = (t.transpose(0, 2, 1, 3) for t in (q_nope, q_pe, k_nope, v))
    scale = (QK_NOPE + QK_ROPE) ** -0.5
    outs = []
    for blk in range(S // Q_BLOCK):
        s0 = blk * Q_BLOCK
        s1 = s0 + Q_BLOCK
        sc = (jnp.einsum('bhqd,bhkd->bhqk', q_nope[:, :, s0:s1], k_nope[:, :, :s1],
                         preferred_element_type=jnp.float32)
              + jnp.einsum('bhqd,bkd->bhqk', q_pe[:, :, s0:s1], k_pe[:, :s1],
                           preferred_element_type=jnp.float32)) * scale
        causal = jnp.arange(s1)[None, :] <= jnp.arange(s0, s1)[:, None]
        p = jax.nn.softmax(jnp.where(causal, sc, -jnp.inf), axis=-1).astype(v.dtype)
        outs.append(jnp.einsum('bhqk,bhkd->bhqd', p, v[:, :, :s1]))
    o = jnp.concatenate(outs, axis=2).transpose(0, 2, 1, 3).reshape(B, S, MLA_HEADS * V_HEAD)
    return o @ w_o


def causal_depthwise_conv(x, w):
    K, C = w.shape
    return lax.conv_general_dilated(x, w[:, None, :].astype(x.dtype), window_strides=(1,),
                                    padding=[(K - 1, 0)], dimension_numbers=('NWC', 'WIO', 'NWC'),
                                    feature_group_count=C)


def chunked_gated_delta_rule(q, k, v, g, beta):
    B, S, H, DK = q.shape
    DV = v.shape[-1]
    N = S // CHUNK

    def chunks(t):
        return jnp.moveaxis(t.reshape(B, N, CHUNK, H, *t.shape[3:]), 3, 2)

    q, k, v, g, beta = (chunks(t) for t in (q, k, v, g, beta))
    g = jnp.cumsum(g, axis=-1)
    g_last = g[..., -1]
    incl = jnp.tril(jnp.ones((CHUNK, CHUNK), dtype=bool))
    strict = jnp.tril(jnp.ones((CHUNK, CHUNK), dtype=bool), -1)
    diff = g[..., :, None] - g[..., None, :]
    decay = jnp.where(incl, jnp.exp(jnp.where(incl, diff, 0.0)), 0.0)
    k_beta = k * beta[..., None]
    v_beta = v * beta[..., None]
    lower = jnp.where(strict, jnp.einsum('bnhid,bnhjd->bnhij', k_beta, k) * decay, 0.0)
    eye = jnp.eye(CHUNK, dtype=q.dtype)
    t_inv = lax.linalg.triangular_solve(eye + lower, jnp.broadcast_to(eye, lower.shape),
                                        left_side=True, lower=True, unit_diagonal=True)
    u = t_inv @ v_beta
    w = t_inv @ (k_beta * jnp.exp(g)[..., None])
    a_intra = jnp.where(incl, jnp.einsum('bnhid,bnhjd->bnhij', q, k) * decay, 0.0)
    q_dec = q * jnp.exp(g)[..., None]
    k_dec = k * jnp.exp(g_last[..., None] - g)[..., None]

    def step(state, xs):
        w_c, u_c, q_c, k_c, a_c, gl_c = xs
        v_new = u_c - w_c @ state
        o_c = q_c @ state + a_c @ v_new
        state = state * jnp.exp(gl_c)[..., None, None] + jnp.swapaxes(k_c, -1, -2) @ v_new
        return state, o_c

    xs = tuple(jnp.moveaxis(t, 1, 0) for t in (w, u, q_dec, k_dec, a_intra, g_last))
    state0 = jnp.zeros((B, H, DK, DV), q.dtype)
    _, o = lax.scan(step, state0, xs)
    return jnp.moveaxis(o, 0, 1).transpose(0, 1, 3, 2, 4).reshape(B, S, H, DV)


def gated_deltanet(h, w_in, conv_w, a_log, dt_bias, norm_w, w_o):
    B, S, _ = h.shape
    proj = h @ w_in
    qkv, z, b_logit, a_in = jnp.split(proj, [QKV_DIM, QKV_DIM + GDN_VAL, QKV_DIM + GDN_VAL + GDN_HEADS], axis=-1)
    qkv = jax.nn.silu(causal_depthwise_conv(qkv, conv_w))
    q, k, v = jnp.split(qkv, [GDN_HEADS * GDN_DK, 2 * GDN_HEADS * GDN_DK], axis=-1)
    q = l2norm(q.reshape(B, S, GDN_HEADS, GDN_DK)) * (GDN_DK ** -0.5)
    k = l2norm(k.reshape(B, S, GDN_HEADS, GDN_DK))
    v = v.reshape(B, S, GDN_HEADS, GDN_DV).astype(jnp.float32)
    beta = jax.nn.sigmoid(b_logit.astype(jnp.float32))
    g = -jnp.exp(a_log.astype(jnp.float32)) * jax.nn.softplus(a_in.astype(jnp.float32) + dt_bias.astype(jnp.float32))
    o = chunked_gated_delta_rule(q, k, v, g, beta)
    o = rms_norm(o, norm_w) * jax.nn.silu(z.reshape(B, S, GDN_HEADS, GDN_DV).astype(jnp.float32))
    return o.reshape(B, S, GDN_VAL).astype(h.dtype) @ w_o


def moe_swiglu(h, router_w, router_b, w_gate, w_up, w_down):
    B, S, D = h.shape
    T = B * S
    A = T * TOP_K
    NB = -(-A // MOE_BLOCK) + N_EXPERTS
    xt = h.reshape(T, D)
    logits = jnp.einsum('td,de->te', xt, router_w, preferred_element_type=jnp.float32) + router_b.astype(jnp.float32)
    top_logit, top_idx = lax.top_k(logits, TOP_K)
    gates = jax.nn.softmax(top_logit, axis=-1).astype(h.dtype)
    flat_e = top_idx.reshape(A)
    flat_tok = jnp.arange(A, dtype=jnp.int32) // TOP_K
    order = jnp.argsort(flat_e)
    e_sorted = flat_e[order]
    tok_sorted = flat_tok[order]
    g_sorted = gates.reshape(A)[order]
    counts = jnp.bincount(flat_e, length=N_EXPERTS)
    padded = (counts + MOE_BLOCK - 1) // MOE_BLOCK * MOE_BLOCK
    pad_end = jnp.cumsum(padded)
    pad_start = pad_end - padded
    grp_start = jnp.cumsum(counts) - counts
    dest = pad_start[e_sorted] + (jnp.arange(A, dtype=jnp.int32) - grp_start[e_sorted])
    slot_tok = jnp.zeros((NB * MOE_BLOCK,), jnp.int32).at[dest].set(tok_sorted)
    slot_gate = jnp.zeros((NB * MOE_BLOCK,), gates.dtype).at[dest].set(g_sorted)
    block_expert = jnp.minimum(jnp.searchsorted(pad_end, jnp.arange(NB) * MOE_BLOCK, side='right'), N_EXPERTS - 1)
    xb = xt[slot_tok].reshape(NB, MOE_BLOCK, D)

    def expert_block(args):
        xe, e = args
        return swiglu(xe, w_gate[e], w_up[e], w_down[e])

    yb = lax.map(expert_block, (xb, block_expert)).reshape(NB * MOE_BLOCK, D)
    y = jax.ops.segment_sum(yb * slot_gate[:, None], slot_tok, num_segments=T)
    return y.reshape(B, S, D)


def _normal(key, shape, fan_in):
    return jax.random.normal(key, shape, jnp.float32) * (fan_in ** -0.5)


def _gain(key, shape):
    return 1.0 + 0.02 * jax.random.normal(key, shape, jnp.float32)


def setup_inputs(seed: int = 0) -> dict:
    key = jax.random.key(seed)
    ks = jax.random.split(key, 32)
    D = D_MODEL
    Lm, Lg = N_MLA_LAYERS, N_GDN_LAYERS
    dt = jax.random.uniform(ks[17], (Lg, GDN_HEADS), jnp.float32, minval=0.001, maxval=0.1)
    return {
        'x': jax.random.normal(ks[0], (BATCH, SEQ, D), jnp.float32),
        'positions': jnp.arange(SEQ, dtype=jnp.int32)[None, :] + jax.random.randint(ks[1], (BATCH, 1), 0, 4096, dtype=jnp.int32),
        'ln_mix_mla': _gain(ks[2], (Lm, D)),
        'mla_w_in': _normal(ks[3], (Lm, D, MLA_IN), D),
        'mla_q_norm': _gain(ks[4], (Lm, Q_LORA)),
        'mla_w_qb': _normal(ks[5], (Lm, Q_LORA, MLA_HEADS * (QK_NOPE + QK_ROPE)), Q_LORA),
        'mla_kv_norm': _gain(ks[6], (Lm, KV_LORA)),
        'mla_w_kvb': _normal(ks[7], (Lm, KV_LORA, MLA_HEADS * (QK_NOPE + V_HEAD)), KV_LORA),
        'mla_w_o': _normal(ks[8], (Lm, MLA_HEADS * V_HEAD, D), MLA_HEADS * V_HEAD),
        'ln_ffn_dense': _gain(ks[9], (Lm, D)),
        'ffn_w_gate': _normal(ks[10], (Lm, D, D_FF), D),
        'ffn_w_up': _normal(ks[11], (Lm, D, D_FF), D),
        'ffn_w_down': _normal(ks[12], (Lm, D_FF, D), D_FF),
        'ln_mix_gdn': _gain(ks[13], (Lg, D)),
        'gdn_w_in': _normal(ks[14], (Lg, D, GDN_IN), D),
        'gdn_conv_w': _normal(ks[15], (Lg, CONV_WIDTH, QKV_DIM), CONV_WIDTH),
        'gdn_a_log': jnp.log(jax.random.uniform(ks[16], (Lg, GDN_HEADS), jnp.float32, minval=1.0, maxval=16.0)),
        'gdn_dt_bias': jnp.log(jnp.expm1(dt)),
        'gdn_norm': _gain(ks[18], (Lg, GDN_DV)),
        'gdn_w_o': _normal(ks[19], (Lg, GDN_VAL, D), GDN_VAL),
        'ln_ffn_moe': _gain(ks[20], (Lg, D)),
        'moe_router': _normal(ks[21], (Lg, D, N_EXPERTS), D),
        'moe_router_bias': 0.01 * jax.random.normal(ks[22], (Lg, N_EXPERTS), jnp.float32),
        'moe_w_gate': _normal(ks[23], (Lg, N_EXPERTS, D, D_FF_EXPERT), D),
        'moe_w_up': _normal(ks[24], (Lg, N_EXPERTS, D, D_FF_EXPERT), D),
        'moe_w_down': _normal(ks[25], (Lg, N_EXPERTS, D_FF_EXPERT, D), D_FF_EXPERT),
        'final_norm': _gain(ks[26], (D,)),
    }


def reference(x, positions, ln_mix_mla, mla_w_in, mla_q_norm, mla_w_qb, mla_kv_norm, mla_w_kvb, mla_w_o,
              ln_ffn_dense, ffn_w_gate, ffn_w_up, ffn_w_down,
              ln_mix_gdn, gdn_w_in, gdn_conv_w, gdn_a_log, gdn_dt_bias, gdn_norm, gdn_w_o,
              ln_ffn_moe, moe_router, moe_router_bias, moe_w_gate, moe_w_up, moe_w_down, final_norm):
    h = x
    for i in range(DEPTH):
        j = i // N_MIXERS
        if i % N_MIXERS == 0:
            h = h + mla_attention(rms_norm(h, ln_mix_mla[j]), positions, mla_w_in[j], mla_q_norm[j],
                                  mla_w_qb[j], mla_kv_norm[j], mla_w_kvb[j], mla_w_o[j])
            h = h + swiglu(rms_norm(h, ln_ffn_dense[j]), ffn_w_gate[j], ffn_w_up[j], ffn_w_down[j])
        else:
            h = h + gated_deltanet(rms_norm(h, ln_mix_gdn[j]), gdn_w_in[j], gdn_conv_w[j], gdn_a_log[j],
                                   gdn_dt_bias[j], gdn_norm[j], gdn_w_o[j])
            h = h + moe_swiglu(rms_norm(h, ln_ffn_moe[j]), moe_router[j], moe_router_bias[j],
                               moe_w_gate[j], moe_w_up[j], moe_w_down[j])
    return rms_norm(h, final_norm)
```

```python
import functools

import jax
import jax.numpy as jnp
from jax import lax
from jax.experimental import pallas as pl
from jax.experimental.pallas import tpu as pltpu

F32 = jnp.float32
BF16 = jnp.bfloat16

RMS_EPS = 1e-6
ROPE_THETA = 10000.0
QK_NOPE = 128
QK_ROPE = 64
V_HEAD = 128
Q_HEAD_PAD = 256
GDN_DK = 128
GDN_DV = 128
CHUNK = 64
TOP_K = 2
LANES = 128
VMEM_LIMIT = 56 * 1024 * 1024
NEG_BIG = -1e30


def _cparams(n_axes):
    return pltpu.CompilerParams(dimension_semantics=("arbitrary",) * n_axes, vmem_limit_bytes=VMEM_LIMIT)


def _silu(x):
    return x * jax.nn.sigmoid(x)


def _rmsnorm_kernel(x_ref, g_ref, o_ref):
    xf = x_ref[...].astype(F32)
    var = jnp.mean(xf * xf, axis=-1, keepdims=True)
    o_ref[...] = (xf * lax.rsqrt(var + RMS_EPS) * g_ref[...]).astype(o_ref.dtype)


def rmsnorm(x, gain, *, col_blk=0, tm=512, out_dtype=BF16):
    rows = x.shape[0]
    k = gain.shape[-1]
    tm = min(tm, rows)
    return pl.pallas_call(
        _rmsnorm_kernel,
        grid=(rows // tm,),
        in_specs=[pl.BlockSpec((tm, k), lambda i: (i, col_blk)), pl.BlockSpec((1, k), lambda i: (0, 0))],
        out_specs=pl.BlockSpec((tm, k), lambda i: (i, 0)),
        out_shape=jax.ShapeDtypeStruct((rows, k), out_dtype),
        compiler_params=_cparams(1),
        name="rmsnorm",
    )(x, gain.reshape(1, k).astype(F32))


def _gmm_kernel(be_ref, nu_ref, x_ref, *rest, n_w, mode, has_gate, has_res, scale):
    w_refs = rest[:n_w]
    rest = rest[n_w:]
    gate_ref = res_ref = cs_ref = None
    if has_gate:
        gate_ref, rest = rest[0], rest[1:]
    if has_res:
        res_ref, rest = rest[0], rest[1:]
    if mode == "rope":
        cs_ref, rest = rest[0], rest[1:]
    o_ref = rest[0]
    wbf_refs = rest[1:]
    rb = pl.program_id(1)

    prev = be_ref[jnp.maximum(rb - 1, 0)]
    changed = jnp.logical_or(rb == 0, be_ref[rb] != prev)

    @pl.when(changed)
    def _():
        for w_ref, wbf_ref in zip(w_refs, wbf_refs):
            wbf_ref[...] = w_ref[0].astype(BF16)

    @pl.when(rb < nu_ref[0])
    def _():
        x = x_ref[...]
        if mode == "swiglu":
            a = jnp.dot(x, wbf_refs[0][...], preferred_element_type=F32)
            b = jnp.dot(x, wbf_refs[1][...], preferred_element_type=F32)
            o_ref[...] = (_silu(a) * b).astype(o_ref.dtype)
        else:
            acc = jnp.dot(x, wbf_refs[0][...], preferred_element_type=F32)
            if has_gate:
                acc = acc * gate_ref[...]
            if has_res:
                acc = acc + res_ref[...]
            if mode == "rope":
                cs = cs_ref[...]
                for hh in range(acc.shape[1] // Q_HEAD_PAD):
                    c0 = hh * Q_HEAD_PAD
                    o_ref[:, c0:c0 + QK_NOPE] = (acc[:, c0:c0 + QK_NOPE] * scale).astype(o_ref.dtype)
                    g = acc[:, c0 + QK_NOPE:c0 + Q_HEAD_PAD] * cs
                    r = g + pltpu.roll(g, QK_ROPE, axis=1)
                    o_ref[:, c0 + QK_NOPE:c0 + Q_HEAD_PAD] = (r * scale).astype(o_ref.dtype)
            else:
                o_ref[...] = acc.astype(o_ref.dtype)

    @pl.when(rb >= nu_ref[0])
    def _():
        o_ref[...] = jnp.zeros(o_ref.shape, o_ref.dtype)


def gmm(x, ws, *, n_out, tm, tn, out_dtype, be=None, nused=None, mode="plain", gate=None, residual=None,
        cs=None, scale=None, name="gmm"):
    rows, k = x.shape
    tm = min(tm, rows)
    tn = min(tn, n_out)
    nblk = rows // tm
    if be is None:
        be = jnp.zeros((nblk,), jnp.int32)
        nused = jnp.full((1,), nblk, jnp.int32)
    n_w = len(ws)

    def x_map(n, rb, be_r, nu_r):
        return (jnp.minimum(rb, nu_r[0] - 1), 0)

    def w_map(n, rb, be_r, nu_r):
        return (be_r[rb], 0, n)

    def row_map(n, rb, be_r, nu_r):
        return (rb, 0)

    def out_map(n, rb, be_r, nu_r):
        return (rb, n)

    in_specs = [pl.BlockSpec((tm, k), x_map)] + [pl.BlockSpec((1, k, tn), w_map) for _ in ws]
    args = [x] + list(ws)
    if gate is not None:
        in_specs.append(pl.BlockSpec((tm, 1), row_map))
        args.append(gate)
    if residual is not None:
        in_specs.append(pl.BlockSpec((tm, tn), out_map))
        args.append(residual)
    if mode == "rope":
        in_specs.append(pl.BlockSpec((tm, LANES), row_map))
        args.append(cs)
    kern = functools.partial(_gmm_kernel, n_w=n_w, mode=mode, has_gate=gate is not None,
                             has_res=residual is not None, scale=scale)
    return pl.pallas_call(
        kern,
        grid_spec=pltpu.PrefetchScalarGridSpec(
            num_scalar_prefetch=2,
            grid=(n_out // tn, nblk),
            in_specs=in_specs,
            out_specs=pl.BlockSpec((tm, tn), out_map),
            scratch_shapes=[pltpu.VMEM((k, tn), BF16) for _ in ws],
        ),
        out_shape=jax.ShapeDtypeStruct((rows, n_out), out_dtype),
        compiler_params=_cparams(2),
        name=name,
    )(be, nused, *args)


def _kpe_kernel(x_ref, cs_ref, o_ref):
    g = x_ref[...] * cs_ref[...]
    r = g + pltpu.roll(g, QK_ROPE, axis=1)
    lane = lax.broadcasted_iota(jnp.int32, r.shape, 1)
    o_ref[...] = jnp.where(lane < QK_ROPE, r, 0.0).astype(o_ref.dtype)


def rope_shared_key(proj, cs, col_blk, *, tm=512):
    rows = proj.shape[0]
    tm = min(tm, rows)
    return pl.pallas_call(
        _kpe_kernel,
        grid=(rows // tm,),
        in_specs=[pl.BlockSpec((tm, LANES), lambda i: (i, col_blk)), pl.BlockSpec((tm, LANES), lambda i: (i, 0))],
        out_specs=pl.BlockSpec((tm, LANES), lambda i: (i, 0)),
        out_shape=jax.ShapeDtypeStruct((rows, LANES), BF16),
        compiler_params=_cparams(1),
        name="rope_shared_key",
    )(proj, cs)


def _attn_kernel(q_ref, kn_ref, v_ref, kpe_ref, o_ref, kcat_ref, *, tq):
    qi = pl.program_id(2)

    @pl.when(qi == 0)
    def _():
        kcat_ref[:, 0:QK_NOPE] = kn_ref[...]
        kcat_ref[:, QK_NOPE:Q_HEAD_PAD] = kpe_ref[...]

    q = q_ref[...]
    row = lax.broadcasted_iota(jnp.int32, (tq, tq), 0)
    col = lax.broadcasted_iota(jnp.int32, (tq, tq), 1)

    def step(j, carry, masked):
        m, l, acc = carry
        r0 = pl.multiple_of(j * tq, tq)
        k = kcat_ref[pl.ds(r0, tq), :]
        v = v_ref[pl.ds(r0, tq), :]
        s = lax.dot_general(q, k, (((1,), (1,)), ((), ())), preferred_element_type=F32)
        if masked:
            s = jnp.where(col <= row, s, NEG_BIG)
        m_new = jnp.maximum(m, jnp.max(s, axis=1, keepdims=True))
        p = jnp.exp(s - m_new)
        alpha = jnp.exp(m - m_new)
        l = alpha * l + jnp.sum(p, axis=1, keepdims=True)
        acc = alpha * acc + jnp.dot(p.astype(BF16), v, preferred_element_type=F32)
        return m_new, l, acc

    init = (jnp.full((tq, 1), NEG_BIG, F32), jnp.zeros((tq, 1), F32), jnp.zeros((tq, V_HEAD), F32))
    carry = lax.fori_loop(0, qi, lambda j, c: step(j, c, False), init)
    _, l, acc = step(qi, carry, True)
    o_ref[...] = (acc / l).astype(o_ref.dtype)


def mla_attention_core(q, kv, kpe, *, batch, seq, heads, tq=256):
    tq = min(tq, seq)
    nq = seq // tq
    kern = functools.partial(_attn_kernel, tq=tq)
    return pl.pallas_call(
        kern,
        grid=(batch, heads, nq),
        in_specs=[
            pl.BlockSpec((tq, Q_HEAD_PAD), lambda b, h, i: (b * nq + i, h)),
            pl.BlockSpec((seq, QK_NOPE), lambda b, h, i: (b, 2 * h)),
            pl.BlockSpec((seq, V_HEAD), lambda b, h, i: (b, 2 * h + 1)),
            pl.BlockSpec((seq, LANES), lambda b, h, i: (b, 0)),
        ],
        out_specs=pl.BlockSpec((tq, V_HEAD), lambda b, h, i: (b * nq + i, h)),
        out_shape=jax.ShapeDtypeStruct((batch * seq, heads * V_HEAD), BF16),
        scratch_shapes=[pltpu.VMEM((seq, Q_HEAD_PAD), BF16)],
        compiler_params=_cparams(3),
        name="mla_flash_attention",
    )(q, kv, kv, kpe)


def _gdn_gate_kernel(x_ref, w_ref, alog_ref, dtb_ref, o_ref, *, heads):
    ba = jnp.dot(x_ref[...], w_ref[...].astype(BF16), preferred_element_type=F32)
    lane = lax.broadcasted_iota(jnp.int32, ba.shape, 1)
    beta = jax.nn.sigmoid(ba)
    z = ba + dtb_ref[...]
    softplus = jnp.maximum(z, 0.0) + jnp.log1p(jnp.exp(-jnp.abs(z)))
    g = -jnp.exp(alog_ref[...]) * softplus
    o_ref[...] = jnp.where(lane < heads, beta, g)


def gdn_gates(xn, w_ba, a_log, dt_bias, *, heads, tm=512):
    rows, k = xn.shape
    tm = min(tm, rows)
    pad = LANES - 2 * heads
    w = jnp.pad(w_ba, ((0, 0), (0, pad)))
    alog = jnp.pad(a_log.astype(F32), (heads, pad)).reshape(1, LANES)
    dtb = jnp.pad(dt_bias.astype(F32), (heads, pad)).reshape(1, LANES)
    return pl.pallas_call(
        functools.partial(_gdn_gate_kernel, heads=heads),
        grid=(rows // tm,),
        in_specs=[pl.BlockSpec((tm, k), lambda i: (i, 0)), pl.BlockSpec((k, LANES), lambda i: (0, 0)),
                  pl.BlockSpec((1, LANES), lambda i: (0, 0)), pl.BlockSpec((1, LANES), lambda i: (0, 0))],
        out_specs=pl.BlockSpec((tm, LANES), lambda i: (i, 0)),
        out_shape=jax.ShapeDtypeStruct((rows, LANES), F32),
        compiler_params=_cparams(1),
        name="gdn_gates",
    )(xn, w, alog, dtb)


def _nt(a, b):
    return lax.dot_general(a, b, (((1,), (1,)), ((), ())), preferred_element_type=F32)


def _gdn_kernel(q_ref, k_ref, v_ref, z_ref, cwq_ref, cwk_ref, cwv_ref, grow_ref, gcol_ref, brow_ref, bcol_ref,
                nw_ref, o_ref, qs, ks, vs, gcr, gcc, st, *, hg, seq):
    nchunk = seq // CHUNK
    row_s = lax.broadcasted_iota(jnp.int32, (seq, LANES), 0)

    def conv_silu(x, w):
        y = x * w[3:4, :]
        for s in (1, 2, 3):
            y = y + jnp.where(row_s >= s, pltpu.roll(x, s, axis=0), 0.0) * w[3 - s:4 - s, :]
        return _silu(y)

    ii = lax.broadcasted_iota(jnp.int32, (CHUNK, CHUNK), 0)
    jj = lax.broadcasted_iota(jnp.int32, (CHUNK, CHUNK), 1)
    incl = ii >= jj
    strict = ii > jj
    upper_ones = jnp.where(ii <= jj, 1.0, 0.0).astype(F32)
    lower_ones = jnp.where(incl, 1.0, 0.0).astype(F32)
    eye = jnp.where(ii == jj, 1.0, 0.0).astype(F32)

    for hh in range(hg):
        c0 = hh * LANES
        q = conv_silu(q_ref[:, c0:c0 + LANES], cwq_ref[:, c0:c0 + LANES])
        qs[hh] = q * lax.rsqrt(jnp.sum(q * q, axis=-1, keepdims=True) + RMS_EPS) * (GDN_DK ** -0.5)
        k = conv_silu(k_ref[:, c0:c0 + LANES], cwk_ref[:, c0:c0 + LANES])
        ks[hh] = k * lax.rsqrt(jnp.sum(k * k, axis=-1, keepdims=True) + RMS_EPS)
        vs[hh] = conv_silu(v_ref[:, c0:c0 + LANES], cwv_ref[:, c0:c0 + LANES])
        gcr[hh] = jnp.dot(grow_ref[0, hh], upper_ones, preferred_element_type=F32, precision=lax.Precision.HIGHEST)
        gcc[hh] = jnp.dot(lower_ones, gcol_ref[0, hh], preferred_element_type=F32, precision=lax.Precision.HIGHEST)
        st[hh] = jnp.zeros((GDN_DK, GDN_DV), F32)

    lane_c = lax.broadcasted_iota(jnp.int32, (CHUNK, nchunk), 1)
    nw = nw_ref[...]

    def chunk_body(c, carry):
        r0 = pl.multiple_of(c * CHUNK, CHUNK)
        for hh in range(hg):
            c0 = hh * LANES
            q = qs[hh, pl.ds(r0, CHUNK), :]
            k = ks[hh, pl.ds(r0, CHUNK), :]
            v = vs[hh, pl.ds(r0, CHUNK), :]
            gc_row = gcr[hh, pl.ds(c, 1), :]
            b_row = brow_ref[0, hh, pl.ds(c, 1), :]
            sel = lane_c == c
            gc_col = jnp.sum(jnp.where(sel, gcc[hh], 0.0), axis=1, keepdims=True)
            b_col = jnp.sum(jnp.where(sel, bcol_ref[0, hh], 0.0), axis=1, keepdims=True)
            gl = gc_row[:, CHUNK - 1:CHUNK]
            decay = jnp.where(incl, jnp.exp(jnp.where(incl, gc_col - gc_row, 0.0)), 0.0)
            k16 = k.astype(BF16)
            q16 = q.astype(BF16)
            v16 = v.astype(BF16)
            lower = jnp.where(strict, _nt(k16, k16) * decay * b_col, 0.0)
            p = -lower
            t_inv = eye + p
            for _ in range(5):
                p16 = p.astype(BF16)
                p = jnp.dot(p16, p16, preferred_element_type=F32)
                t_inv = t_inv + jnp.dot(t_inv.astype(BF16), p.astype(BF16), preferred_element_type=F32)
            u = jnp.dot((t_inv * b_row).astype(BF16), v16, preferred_element_type=F32)
            w = jnp.dot((t_inv * (b_row * jnp.exp(gc_row))).astype(BF16), k16, preferred_element_type=F32)
            a = jnp.where(incl, _nt(q16, k16) * decay, 0.0)
            s = st[hh]
            s16 = s.astype(BF16)
            v_new = u - jnp.dot(w.astype(BF16), s16, preferred_element_type=F32)
            vn16 = v_new.astype(BF16)
            o = jnp.exp(gc_col) * jnp.dot(q16, s16, preferred_element_type=F32) + jnp.dot(
                a.astype(BF16), vn16, preferred_element_type=F32)
            kd16 = (k * jnp.exp(gl - gc_col)).astype(BF16)
            st[hh] = s * jnp.exp(gl) + lax.dot_general(kd16, vn16, (((0,), (0,)), ((), ())),
                                                       preferred_element_type=F32)
            on = o * lax.rsqrt(jnp.mean(o * o, axis=-1, keepdims=True) + RMS_EPS) * nw
            zz = z_ref[pl.ds(r0, CHUNK), c0:c0 + LANES]
            o_ref[pl.ds(r0, CHUNK), c0:c0 + LANES] = (on * _silu(zz)).astype(o_ref.dtype)
        return carry

    lax.fori_loop(0, nchunk, chunk_body, 0)


def gdn_delta(proj, conv_w, g_rows, g_cols, b_rows, b_cols, norm_w, *, batch, seq, heads, hg=2):
    hg = min(hg, heads)
    nhb = heads // hg
    wblk = hg * LANES
    nchunk = seq // CHUNK
    kern = functools.partial(_gdn_kernel, hg=hg, seq=seq)

    def pmap(off):
        return lambda b, h: (b, off * nhb + h)

    def cmap(off):
        return lambda b, h: (0, off * nhb + h)

    gspec_r = pl.BlockSpec((1, hg, nchunk, CHUNK), lambda b, h: (b, h, 0, 0))
    gspec_c = pl.BlockSpec((1, hg, CHUNK, nchunk), lambda b, h: (b, h, 0, 0))
    return pl.pallas_call(
        kern,
        grid=(batch, nhb),
        in_specs=[pl.BlockSpec((seq, wblk), pmap(0)), pl.BlockSpec((seq, wblk), pmap(1)),
                  pl.BlockSpec((seq, wblk), pmap(2)), pl.BlockSpec((seq, wblk), pmap(3)),
                  pl.BlockSpec((4, wblk), cmap(0)), pl.BlockSpec((4, wblk), cmap(1)), pl.BlockSpec((4, wblk), cmap(2)),
                  gspec_r, gspec_c, gspec_r, gspec_c,
                  pl.BlockSpec((1, LANES), lambda b, h: (0, 0))],
        out_specs=pl.BlockSpec((seq, wblk), lambda b, h: (b, h)),
        out_shape=jax.ShapeDtypeStruct((batch * seq, heads * GDN_DV), BF16),
        scratch_shapes=[pltpu.VMEM((hg, seq, LANES), F32), pltpu.VMEM((hg, seq, LANES), F32),
                        pltpu.VMEM((hg, seq, LANES), F32), pltpu.VMEM((hg, nchunk, CHUNK), F32),
                        pltpu.VMEM((hg, CHUNK, nchunk), F32), pltpu.VMEM((hg, GDN_DK, GDN_DV), F32)],
        compiler_params=_cparams(2),
        name="gdn_delta_rule",
    )(proj, proj, proj, proj, conv_w, conv_w, conv_w, g_rows, g_cols, b_rows, b_cols,
      norm_w.reshape(1, LANES).astype(F32))


def _router_kernel(x_ref, g_ref, w_ref, b_ref, o_ref):
    xf = x_ref[...]
    var = jnp.mean(xf * xf, axis=-1, keepdims=True)
    xn = xf * lax.rsqrt(var + RMS_EPS) * g_ref[...]
    o_ref[...] = jnp.dot(xn, w_ref[...], preferred_element_type=F32, precision=lax.Precision.HIGHEST) + b_ref[...]


def router_logits(h, gain, router_w, router_b, *, tm=512):
    rows, d = h.shape
    tm = min(tm, rows)
    n_e = router_w.shape[1]
    w = jnp.pad(router_w, ((0, 0), (0, LANES - n_e)))
    b = jnp.pad(router_b.astype(F32), (0, LANES - n_e)).reshape(1, LANES)
    return pl.pallas_call(
        _router_kernel,
        grid=(rows // tm,),
        in_specs=[pl.BlockSpec((tm, d), lambda i: (i, 0)), pl.BlockSpec((1, d), lambda i: (0, 0)),
                  pl.BlockSpec((d, LANES), lambda i: (0, 0)), pl.BlockSpec((1, LANES), lambda i: (0, 0))],
        out_specs=pl.BlockSpec((tm, LANES), lambda i: (i, 0)),
        out_shape=jax.ShapeDtypeStruct((rows, LANES), F32),
        compiler_params=_cparams(1),
        name="moe_router",
    )(h, gain.reshape(1, d).astype(F32), w, b)


def _row_copy(src_hbm, dst, sem, src_row, dst_row):
    return pltpu.make_async_copy(src_hbm.at[pl.ds(src_row, 1)], dst.at[pl.ds(dst_row, 1)], sem)


def _gather_norm_kernel(tok_ref, h_hbm, g_ref, o_ref, buf, sem, *, tm, nblk):
    rb = pl.program_id(0)

    def issue(blk, slot):
        def one(r, carry):
            _row_copy(h_hbm, buf.at[slot], sem.at[slot], tok_ref[blk * tm + r], r).start()
            return carry
        lax.fori_loop(0, tm, one, 0)

    @pl.when(rb == 0)
    def _():
        issue(0, 0)

    @pl.when(rb + 1 < nblk)
    def _():
        issue(rb + 1, (rb + 1) % 2)

    slot = rb % 2
    pltpu.make_async_copy(h_hbm.at[pl.ds(0, tm)], buf.at[slot], sem.at[slot]).wait()
    xf = buf[slot]
    var = jnp.mean(xf * xf, axis=-1, keepdims=True)
    o_ref[...] = (xf * lax.rsqrt(var + RMS_EPS) * g_ref[...]).astype(o_ref.dtype)


def gather_norm(h, gain, slot_tok, *, tm):
    rows, d = h.shape
    n_slots = slot_tok.shape[0]
    nblk = n_slots // tm
    kern = functools.partial(_gather_norm_kernel, tm=tm, nblk=nblk)
    return pl.pallas_call(
        kern,
        grid_spec=pltpu.PrefetchScalarGridSpec(
            num_scalar_prefetch=1,
            grid=(nblk,),
            in_specs=[pl.BlockSpec(memory_space=pl.ANY), pl.BlockSpec((1, d), lambda i, t: (0, 0))],
            out_specs=pl.BlockSpec((tm, d), lambda i, t: (i, 0)),
            scratch_shapes=[pltpu.VMEM((2, tm, d), F32), pltpu.SemaphoreType.DMA((2,))],
        ),
        out_shape=jax.ShapeDtypeStruct((n_slots, d), BF16),
        compiler_params=_cparams(1),
        name="moe_gather_norm",
    )(slot_tok, h, gain.reshape(1, d).astype(F32))


def _combine_kernel(slot_ref, y_hbm, h_ref, g_ref, o_ref, buf, sem, *, tc, nblk):
    i = pl.program_id(0)

    def issue(blk, slot):
        def one(r, carry):
            for kk in range(TOP_K):
                _row_copy(y_hbm, buf.at[slot, kk], sem.at[slot], slot_ref[(blk * tc + r) * TOP_K + kk], r).start()
            return carry
        lax.fori_loop(0, tc, one, 0)

    @pl.when(i == 0)
    def _():
        issue(0, 0)

    @pl.when(i + 1 < nblk)
    def _():
        issue(i + 1, (i + 1) % 2)

    slot = i % 2
    for kk in range(TOP_K):
        pltpu.make_async_copy(y_hbm.at[pl.ds(0, tc)], buf.at[slot, kk], sem.at[slot]).wait()
    xf = h_ref[...] + buf[slot, 0] + buf[slot, 1]
    var = jnp.mean(xf * xf, axis=-1, keepdims=True)
    o_ref[...] = xf * lax.rsqrt(var + RMS_EPS) * g_ref[...]


def combine_norm(h, y, tok_slots, final_gain, *, tc=256):
    rows, d = h.shape
    tc = min(tc, rows)
    nblk = rows // tc
    kern = functools.partial(_combine_kernel, tc=tc, nblk=nblk)
    return pl.pallas_call(
        kern,
        grid_spec=pltpu.PrefetchScalarGridSpec(
            num_scalar_prefetch=1,
            grid=(nblk,),
            in_specs=[pl.BlockSpec(memory_space=pl.ANY), pl.BlockSpec((tc, d), lambda i, s: (i, 0)),
                      pl.BlockSpec((1, d), lambda i, s: (0, 0))],
            out_specs=pl.BlockSpec((tc, d), lambda i, s: (i, 0)),
            scratch_shapes=[pltpu.VMEM((2, TOP_K, tc, d), F32), pltpu.SemaphoreType.DMA((2,))],
        ),
        out_shape=jax.ShapeDtypeStruct((rows, d), F32),
        compiler_params=_cparams(1),
        name="moe_combine_norm",
    )(tok_slots, y, h, final_gain.reshape(1, d).astype(F32))


def _route(logits, n_experts, tm):
    n_tok = logits.shape[0]
    n_asg = n_tok * TOP_K
    nblk = n_asg // tm + n_experts
    top_logit, top_idx = lax.top_k(logits[:, :n_experts], TOP_K)
    gates = jax.nn.softmax(top_logit, axis=-1)
    flat_e = top_idx.reshape(n_asg).astype(jnp.int32)
    onehot = (flat_e[:, None] == jnp.arange(n_experts, dtype=jnp.int32)[None, :]).astype(jnp.int32)
    csum = jnp.cumsum(onehot, axis=0)
    rank = jnp.sum(csum * onehot, axis=1) - 1
    counts = csum[-1]
    padded = (counts + tm - 1) // tm * tm
    pad_end = jnp.cumsum(padded)
    pad_start = pad_end - padded
    dest = (pad_start[flat_e] + rank).astype(jnp.int32)
    slot_tok = jnp.zeros((nblk * tm,), jnp.int32).at[dest].set(jnp.arange(n_asg, dtype=jnp.int32) // TOP_K)
    slot_gate = jnp.zeros((nblk * tm,), F32).at[dest].set(gates.reshape(n_asg))
    nused = (pad_end[-1] // tm).astype(jnp.int32)
    blk = jnp.arange(nblk, dtype=jnp.int32)
    be = jnp.minimum(jnp.searchsorted(pad_end, blk * tm, side="right"), n_experts - 1).astype(jnp.int32)
    be = jnp.where(blk < nused, be, be[jnp.maximum(nused - 1, 0)])
    return slot_tok, slot_gate.reshape(-1, 1), dest, be, nused.reshape(1)


def _rope_table(positions):
    inv_freq = ROPE_THETA ** (-jnp.arange(0, QK_ROPE, 2, dtype=F32) / QK_ROPE)
    ang = positions.astype(F32)[..., None] * inv_freq
    cos, sin = jnp.cos(ang), jnp.sin(ang)
    return jnp.concatenate([cos, cos, sin, sin], axis=-1).reshape(-1, 2 * QK_ROPE)


def _rot_cols(w):
    half = w.shape[-1] // 2
    return jnp.concatenate([-w[..., half:], w[..., :half]], axis=-1)


def _mla_layer(h, cs, ln, w_in, q_norm, w_qb, kv_norm, w_kvb, w_o, *, batch, seq):
    d = h.shape[1]
    q_lora = q_norm.shape[0]
    kv_lora = kv_norm.shape[0]
    heads = w_o.shape[0] // V_HEAD
    assert q_lora == kv_lora and q_lora % LANES == 0
    w_rope = w_in[:, q_lora + kv_lora:]
    w_in_p = jnp.concatenate([w_in, _rot_cols(w_rope)], axis=1)
    n_in = w_in_p.shape[1]
    wq = w_qb.reshape(q_lora, heads, QK_NOPE + QK_ROPE)
    wq_p = jnp.concatenate([wq, _rot_cols(wq[..., QK_NOPE:])], axis=-1).reshape(q_lora, heads * Q_HEAD_PAD)

    hn = rmsnorm(h, ln)
    proj = gmm(hn, [w_in_p[None]], n_out=n_in, tm=512, tn=n_in, out_dtype=F32, name="mla_in_proj")
    cqn = rmsnorm(proj, q_norm, col_blk=0)
    ckvn = rmsnorm(proj, kv_norm, col_blk=1)
    scale = (QK_NOPE + QK_ROPE) ** -0.5
    q = gmm(cqn, [wq_p[None]], n_out=heads * Q_HEAD_PAD, tm=512, tn=1024, out_dtype=BF16, mode="rope", cs=cs,
            scale=scale, name="mla_q_proj")
    kv = gmm(ckvn, [w_kvb[None]], n_out=heads * (QK_NOPE + V_HEAD), tm=512, tn=1024, out_dtype=BF16,
             name="mla_kv_proj")
    kpe = rope_shared_key(proj, cs, (q_lora + kv_lora) // LANES)
    o = mla_attention_core(q, kv, kpe, batch=batch, seq=seq, heads=heads)
    return gmm(o, [w_o[None]], n_out=d, tm=512, tn=1024, out_dtype=F32, residual=h, name="mla_out_proj")


def _ffn_layer(h, ln, w_gate, w_up, w_down):
    d = h.shape[1]
    f = w_gate.shape[1]
    hn = rmsnorm(h, ln)
    mid = gmm(hn, [w_gate[None], w_up[None]], n_out=f, tm=512, tn=1024 if f % 1024 == 0 else f // 7,
              out_dtype=BF16, mode="swiglu", name="ffn_up")
    return gmm(mid, [w_down[None]], n_out=d, tm=512, tn=512, out_dtype=F32, residual=h, name="ffn_down")


def _gdn_layer(h, ln, w_in, conv_w, a_log, dt_bias, norm_w, w_o, *, batch, seq):
    d = h.shape[1]
    heads = a_log.shape[0]
    n_main = 4 * heads * LANES
    nchunk = seq // CHUNK
    hn = rmsnorm(h, ln)
    proj = gmm(hn, [w_in[None]], n_out=n_main, tm=512, tn=1024, out_dtype=F32, name="gdn_in_proj")
    gates = gdn_gates(hn, w_in[:, n_main:], a_log, dt_bias, heads=heads)
    beta = gates[:, :heads].reshape(batch, seq, heads).transpose(0, 2, 1).reshape(batch, heads, nchunk, CHUNK)
    g = gates[:, heads:2 * heads].reshape(batch, seq, heads).transpose(0, 2, 1).reshape(batch, heads, nchunk, CHUNK)
    o = gdn_delta(proj, conv_w, g, g.transpose(0, 1, 3, 2), beta, beta.transpose(0, 1, 3, 2), norm_w,
                  batch=batch, seq=seq, heads=heads)
    return gmm(o, [w_o[None]], n_out=d, tm=512, tn=1024, out_dtype=F32, residual=h, name="gdn_out_proj")


def _moe_layer_final(h, ln, router_w, router_b, w_gate, w_up, w_down, final_gain, *, tm=512):
    rows, d = h.shape
    n_e, _, f = w_gate.shape
    tm = min(tm, rows // 4)
    logits = router_logits(h, ln, router_w, router_b)
    slot_tok, slot_gate, dest, be, nused = _route(logits, n_e, tm)
    xs = gather_norm(h, ln, slot_tok, tm=tm)
    mid = gmm(xs, [w_gate, w_up], n_out=f, tm=tm, tn=1024 if f % 1024 == 0 else f // 7, out_dtype=BF16,
              be=be, nused=nused, mode="swiglu", name="moe_up")
    y = gmm(mid, [w_down], n_out=d, tm=tm, tn=512, out_dtype=F32, be=be, nused=nused, gate=slot_gate,
            name="moe_down")
    return combine_norm(h, y, dest, final_gain)


def kernel(x, positions, ln_mix_mla, mla_w_in, mla_q_norm, mla_w_qb, mla_kv_norm, mla_w_kvb, mla_w_o, ln_ffn_dense, ffn_w_gate, ffn_w_up, ffn_w_down, ln_mix_gdn, gdn_w_in, gdn_conv_w, gdn_a_log, gdn_dt_bias, gdn_norm, gdn_w_o, ln_ffn_moe, moe_router, moe_router_bias, moe_w_gate, moe_w_up, moe_w_down, final_norm):
    batch, seq, d = x.shape
    assert ln_mix_mla.shape[0] == 1 and ln_mix_gdn.shape[0] == 1, "two-layer trunk: one MLA and one DeltaNet layer"
    h = x.reshape(batch * seq, d)
    cs = _rope_table(positions)
    h = _mla_layer(h, cs, ln_mix_mla[0], mla_w_in[0], mla_q_norm[0], mla_w_qb[0], mla_kv_norm[0], mla_w_kvb[0],
                   mla_w_o[0], batch=batch, seq=seq)
    h = _ffn_layer(h, ln_ffn_dense[0], ffn_w_gate[0], ffn_w_up[0], ffn_w_down[0])
    h = _gdn_layer(h, ln_mix_gdn[0], gdn_w_in[0], gdn_conv_w[0], gdn_a_log[0], gdn_dt_bias[0], gdn_norm[0],
                   gdn_w_o[0], batch=batch, seq=seq)
    out = _moe_layer_final(h, ln_ffn_moe[0], moe_router[0], moe_router_bias[0], moe_w_gate[0], moe_w_up[0],
                           moe_w_down[0], final_norm)
    return out.reshape(batch, seq, d)
```

```python
import functools

import jax
import jax.numpy as jnp
from jax import lax
from jax.experimental import pallas as pl
from jax.experimental.pallas import tpu as pltpu

F32 = jnp.float32
BF16 = jnp.bfloat16

RMS_EPS = 1e-6
ROPE_THETA = 10000.0
QK_NOPE = 128
QK_ROPE = 64
V_HEAD = 128
Q_HEAD_PAD = 256
GDN_DK = 128
GDN_DV = 128
CHUNK = 64
TOP_K = 2
LANES = 128
VMEM_LIMIT = 56 * 1024 * 1024
NEG_BIG = -1e30


def _cparams(n_axes):
    return pltpu.CompilerParams(dimension_semantics=("arbitrary",) * n_axes, vmem_limit_bytes=VMEM_LIMIT)


def _silu(x):
    return x * jax.nn.sigmoid(x)


def _rmsnorm_kernel(x_ref, g_ref, o_ref):
    xf = x_ref[...].astype(F32)
    var = jnp.mean(xf * xf, axis=-1, keepdims=True)
    o_ref[...] = (xf * lax.rsqrt(var + RMS_EPS) * g_ref[...]).astype(o_ref.dtype)


def rmsnorm(x, gain, *, col_blk=0, tm=512, out_dtype=BF16):
    rows = x.shape[0]
    k = gain.shape[-1]
    tm = min(tm, rows)
    return pl.pallas_call(
        _rmsnorm_kernel,
        grid=(rows // tm,),
        in_specs=[pl.BlockSpec((tm, k), lambda i: (i, col_blk)), pl.BlockSpec((1, k), lambda i: (0, 0))],
        out_specs=pl.BlockSpec((tm, k), lambda i: (i, 0)),
        out_shape=jax.ShapeDtypeStruct((rows, k), out_dtype),
        compiler_params=_cparams(1),
        name="rmsnorm",
    )(x, gain.reshape(1, k).astype(F32))


def _gmm_kernel(be_ref, nu_ref, x_ref, *rest, n_w, mode, has_gate, has_res, scale):
    w_refs = rest[:n_w]
    rest = rest[n_w:]
    gate_ref = res_ref = cs_ref = None
    if has_gate:
        gate_ref, rest = rest[0], rest[1:]
    if has_res:
        res_ref, rest = rest[0], rest[1:]
    if mode == "rope":
        cs_ref, rest = rest[0], rest[1:]
    o_ref = rest[0]
    wbf_refs = rest[1:]
    rb = pl.program_id(1)

    prev = be_ref[jnp.maximum(rb - 1, 0)]
    changed = jnp.logical_or(rb == 0, be_ref[rb] != prev)

    @pl.when(changed)
    def _():
        for w_ref, wbf_ref in zip(w_refs, wbf_refs):
            wbf_ref[...] = w_ref[0].astype(BF16)

    @pl.when(rb < nu_ref[0])
    def _():
        x = x_ref[...]
        if mode == "swiglu":
            a = jnp.dot(x, wbf_refs[0][...], preferred_element_type=F32)
            b = jnp.dot(x, wbf_refs[1][...], preferred_element_type=F32)
            o_ref[...] = (_silu(a) * b).astype(o_ref.dtype)
        else:
            acc = jnp.dot(x, wbf_refs[0][...], preferred_element_type=F32)
            if has_gate:
                acc = acc * gate_ref[...]
            if has_res:
                acc = acc + res_ref[...]
            if mode == "rope":
                cs = cs_ref[...]
                for hh in range(acc.shape[1] // Q_HEAD_PAD):
                    c0 = hh * Q_HEAD_PAD
                    o_ref[:, c0:c0 + QK_NOPE] = (acc[:, c0:c0 + QK_NOPE] * scale).astype(o_ref.dtype)
                    g = acc[:, c0 + QK_NOPE:c0 + Q_HEAD_PAD] * cs
                    r = g + pltpu.roll(g, QK_ROPE, axis=1)
                    o_ref[:, c0 + QK_NOPE:c0 + Q_HEAD_PAD] = (r * scale).astype(o_ref.dtype)
            else:
                o_ref[...] = acc.astype(o_ref.dtype)

    @pl.when(rb >= nu_ref[0])
    def _():
        o_ref[...] = jnp.zeros(o_ref.shape, o_ref.dtype)


def gmm(x, ws, *, n_out, tm, tn, out_dtype, be=None, nused=None, mode="plain", gate=None, residual=None,
        cs=None, scale=None, name="gmm"):
    rows, k = x.shape
    tm = min(tm, rows)
    tn = min(tn, n_out)
    nblk = rows // tm
    if be is None:
        be = jnp.zeros((nblk,), jnp.int32)
        nused = jnp.full((1,), nblk, jnp.int32)
    n_w = len(ws)

    def x_map(n, rb, be_r, nu_r):
        return (jnp.minimum(rb, nu_r[0] - 1), 0)

    def w_map(n, rb, be_r, nu_r):
        return (be_r[rb], 0, n)

    def row_map(n, rb, be_r, nu_r):
        return (rb, 0)

    def out_map(n, rb, be_r, nu_r):
        return (rb, n)

    in_specs = [pl.BlockSpec((tm, k), x_map)] + [pl.BlockSpec((1, k, tn), w_map) for _ in ws]
    args = [x] + list(ws)
    if gate is not None:
        in_specs.append(pl.BlockSpec((tm, 1), row_map))
        args.append(gate)
    if residual is not None:
        in_specs.append(pl.BlockSpec((tm, tn), out_map))
        args.append(residual)
    if mode == "rope":
        in_specs.append(pl.BlockSpec((tm, LANES), row_map))
        args.append(cs)
    kern = functools.partial(_gmm_kernel, n_w=n_w, mode=mode, has_gate=gate is not None,
                             has_res=residual is not None, scale=scale)
    return pl.pallas_call(
        kern,
        grid_spec=pltpu.PrefetchScalarGridSpec(
            num_scalar_prefetch=2,
            grid=(n_out // tn, nblk),
            in_specs=in_specs,
            out_specs=pl.BlockSpec((tm, tn), out_map),
            scratch_shapes=[pltpu.VMEM((k, tn), BF16) for _ in ws],
        ),
        out_shape=jax.ShapeDtypeStruct((rows, n_out), out_dtype),
        compiler_params=_cparams(2),
        name=name,
    )(be, nused, *args)


def _kpe_kernel(x_ref, cs_ref, o_ref):
    g = x_ref[...] * cs_ref[...]
    r = g + pltpu.roll(g, QK_ROPE, axis=1)
    lane = lax.broadcasted_iota(jnp.int32, r.shape, 1)
    o_ref[...] = jnp.where(lane < QK_ROPE, r, 0.0).astype(o_ref.dtype)


def rope_shared_key(proj, cs, col_blk, *, tm=512):
    rows = proj.shape[0]
    tm = min(tm, rows)
    return pl.pallas_call(
        _kpe_kernel,
        grid=(rows // tm,),
        in_specs=[pl.BlockSpec((tm, LANES), lambda i: (i, col_blk)), pl.BlockSpec((tm, LANES), lambda i: (i, 0))],
        out_specs=pl.BlockSpec((tm, LANES), lambda i: (i, 0)),
        out_shape=jax.ShapeDtypeStruct((rows, LANES), BF16),
        compiler_params=_cparams(1),
        name="rope_shared_key",
    )(proj, cs)


def _attn_kernel(q_ref, kn_ref, v_ref, kpe_ref, o_ref, kcat_ref, *, tq):
    qi = pl.program_id(2)

    @pl.when(qi == 0)
    def _():
        kcat_ref[:, 0:QK_NOPE] = kn_ref[...]
        kcat_ref[:, QK_NOPE:Q_HEAD_PAD] = kpe_ref[...]

    q = q_ref[...]
    row = lax.broadcasted_iota(jnp.int32, (tq, tq), 0)
    col = lax.broadcasted_iota(jnp.int32, (tq, tq), 1)

    def step(j, carry, masked):
        m, l, acc = carry
        r0 = pl.multiple_of(j * tq, tq)
        k = kcat_ref[pl.ds(r0, tq), :]
        v = v_ref[pl.ds(r0, tq), :]
        s = lax.dot_general(q, k, (((1,), (1,)), ((), ())), preferred_element_type=F32)
        if masked:
            s = jnp.where(col <= row, s, NEG_BIG)
        m_new = jnp.maximum(m, jnp.max(s, axis=1, keepdims=True))
        p = jnp.exp(s - m_new)
        alpha = jnp.exp(m - m_new)
        l = alpha * l + jnp.sum(p, axis=1, keepdims=True)
        acc = alpha * acc + jnp.dot(p.astype(BF16), v, preferred_element_type=F32)
        return m_new, l, acc

    init = (jnp.full((tq, 1), NEG_BIG, F32), jnp.zeros((tq, 1), F32), jnp.zeros((tq, V_HEAD), F32))
    carry = lax.fori_loop(0, qi, lambda j, c: step(j, c, False), init)
    _, l, acc = step(qi, carry, True)
    o_ref[...] = (acc / l).astype(o_ref.dtype)


def mla_attention_core(q, kv, kpe, *, batch, seq, heads, tq=512):
    tq = min(tq, seq)
    nq = seq // tq
    kern = functools.partial(_attn_kernel, tq=tq)
    return pl.pallas_call(
        kern,
        grid=(batch, heads, nq),
        in_specs=[
            pl.BlockSpec((tq, Q_HEAD_PAD), lambda b, h, i: (b * nq + i, h)),
            pl.BlockSpec((seq, QK_NOPE), lambda b, h, i: (b, 2 * h)),
            pl.BlockSpec((seq, V_HEAD), lambda b, h, i: (b, 2 * h + 1)),
            pl.BlockSpec((seq, LANES), lambda b, h, i: (b, 0)),
        ],
        out_specs=pl.BlockSpec((tq, V_HEAD), lambda b, h, i: (b * nq + i, h)),
        out_shape=jax.ShapeDtypeStruct((batch * seq, heads * V_HEAD), BF16),
        scratch_shapes=[pltpu.VMEM((seq, Q_HEAD_PAD), BF16)],
        compiler_params=_cparams(3),
        name="mla_flash_attention",
    )(q, kv, kv, kpe)


def _gdn_gate_kernel(x_ref, w_ref, alog_ref, dtb_ref, o_ref, *, heads):
    ba = jnp.dot(x_ref[...], w_ref[...].astype(BF16), preferred_element_type=F32)
    lane = lax.broadcasted_iota(jnp.int32, ba.shape, 1)
    beta = jax.nn.sigmoid(ba)
    z = ba + dtb_ref[...]
    softplus = jnp.maximum(z, 0.0) + jnp.log1p(jnp.exp(-jnp.abs(z)))
    g = -jnp.exp(alog_ref[...]) * softplus
    o_ref[...] = jnp.where(lane < heads, beta, g)


def gdn_gates(xn, w_ba, a_log, dt_bias, *, heads, tm=512):
    rows, k = xn.shape
    tm = min(tm, rows)
    pad = LANES - 2 * heads
    w = jnp.pad(w_ba, ((0, 0), (0, pad)))
    alog = jnp.pad(a_log.astype(F32), (heads, pad)).reshape(1, LANES)
    dtb = jnp.pad(dt_bias.astype(F32), (heads, pad)).reshape(1, LANES)
    return pl.pallas_call(
        functools.partial(_gdn_gate_kernel, heads=heads),
        grid=(rows // tm,),
        in_specs=[pl.BlockSpec((tm, k), lambda i: (i, 0)), pl.BlockSpec((k, LANES), lambda i: (0, 0)),
                  pl.BlockSpec((1, LANES), lambda i: (0, 0)), pl.BlockSpec((1, LANES), lambda i: (0, 0))],
        out_specs=pl.BlockSpec((tm, LANES), lambda i: (i, 0)),
        out_shape=jax.ShapeDtypeStruct((rows, LANES), F32),
        compiler_params=_cparams(1),
        name="gdn_gates",
    )(xn, w, alog, dtb)


def _bmm(a, b):
    return lax.dot_general(a, b, (((2,), (1,)), ((0,), (0,))), preferred_element_type=F32)


def _bmm_nt(a, b):
    return lax.dot_general(a, b, (((2,), (2,)), ((0,), (0,))), preferred_element_type=F32)


def _bmm_tn(a, b):
    return lax.dot_general(a, b, (((1,), (1,)), ((0,), (0,))), preferred_element_type=F32)


def _gdn_kernel(q_ref, k_ref, v_ref, z_ref, cwq_ref, cwk_ref, cwv_ref, gate_ref, nw_ref, o_ref,
                halo, lhs_s, bt_s, au_s, egl_s, st, *, hg, ts, heads):
    nc = ts // CHUNK
    hgi = pl.program_id(1)
    first = pl.program_id(2) == 0

    row8 = lax.broadcasted_iota(jnp.int32, (8, LANES), 0)
    row_t = lax.broadcasted_iota(jnp.int32, (ts, LANES), 0)
    lane_t = lax.broadcasted_iota(jnp.int32, (ts, LANES), 1)
    ii = lax.broadcasted_iota(jnp.int32, (CHUNK, CHUNK), 0)
    jj = lax.broadcasted_iota(jnp.int32, (CHUNK, CHUNK), 1)
    incl = ii >= jj
    strict = ii > jj
    lower_ones = jnp.where(incl, 1.0, 0.0).astype(F32)
    eye = jnp.where(ii == jj, 1.0, 0.0).astype(F32)
    gates = gate_ref[...]

    def conv_silu(idx, x_ref, w_ref, c0):
        x = x_ref[:, c0:c0 + LANES]
        w = w_ref[:, c0:c0 + LANES]
        old = jnp.where(first, 0.0, halo[idx, :, c0:c0 + LANES])
        halo[idx, :, c0:c0 + LANES] = x[ts - 8:ts, :]
        y = x * w[3:4, :]
        corr = jnp.zeros((8, LANES), F32)
        for s in (1, 2, 3):
            y = y + jnp.where(row_t >= s, pltpu.roll(x, s, axis=0), 0.0) * w[3 - s:4 - s, :]
            corr = corr + jnp.where(row8 < s, pltpu.roll(old, s, axis=0), 0.0) * w[3 - s:4 - s, :]
        y = jnp.concatenate([y[0:8, :] + corr, y[8:, :]], axis=0)
        return _silu(y)

    for hh in range(hg):
        c0 = hh * LANES
        head = hgi * hg + hh

        @pl.when(first)
        def _():
            st[hh] = jnp.zeros((GDN_DK, GDN_DV), F32)

        q = conv_silu(0, q_ref, cwq_ref, c0)
        q = q * lax.rsqrt(jnp.sum(q * q, axis=-1, keepdims=True) + RMS_EPS) * (GDN_DK ** -0.5)
        k = conv_silu(1, k_ref, cwk_ref, c0)
        k = k * lax.rsqrt(jnp.sum(k * k, axis=-1, keepdims=True) + RMS_EPS)
        v = conv_silu(2, v_ref, cwv_ref, c0)

        b_col = jnp.sum(jnp.where(lane_t == head, gates, 0.0), axis=1, keepdims=True)
        g_col = jnp.sum(jnp.where(lane_t == heads + head, gates, 0.0), axis=1, keepdims=True)
        b3 = b_col.reshape(nc, CHUNK, 1)
        g3 = g_col.reshape(nc, CHUNK, 1)
        g_row3 = jnp.sum(g3 * eye, axis=1, keepdims=True)
        gc3 = jnp.sum(lower_ones * g_row3, axis=2, keepdims=True)
        gc_row3 = jnp.sum(gc3 * eye, axis=1, keepdims=True)
        gl3 = gc3[:, CHUNK - 1:CHUNK, :]
        decay = jnp.where(incl, jnp.exp(jnp.where(incl, gc3 - gc_row3, 0.0)), 0.0)
        egc3 = jnp.exp(gc3)

        q3 = q.reshape(nc, CHUNK, LANES)
        k3 = k.reshape(nc, CHUNK, LANES)
        v3 = v.reshape(nc, CHUNK, LANES)
        q16 = q3.astype(BF16)
        k16 = k3.astype(BF16)
        lower = jnp.where(strict, _bmm_nt(k16, k16) * decay * b3, 0.0)
        p = -lower
        t_inv = eye + p
        for _ in range(5):
            p16 = p.astype(BF16)
            p = _bmm(p16, p16)
            t_inv = t_inv + _bmm(t_inv.astype(BF16), p.astype(BF16))
        vk16 = jnp.concatenate([(v3 * b3).astype(BF16), (k3 * (b3 * egc3)).astype(BF16)], axis=-1)
        uw16 = _bmm(t_inv.astype(BF16), vk16).astype(BF16)
        a16 = jnp.where(incl, _bmm_nt(q16, k16) * decay, 0.0).astype(BF16)
        awu = _bmm(a16, uw16)
        kd16 = (k3 * jnp.exp(gl3 - gc3)).astype(BF16)
        mb = _bmm_tn(kd16, uw16)
        lhs_s[hh, :, 0:GDN_DK, :] = mb[:, :, LANES:].astype(BF16)
        lhs_s[hh, :, GDN_DK:GDN_DK + CHUNK, :] = (q3 * egc3 - awu[:, :, LANES:]).astype(BF16)
        bt_s[hh] = mb[:, :, 0:LANES]
        au_s[hh] = awu[:, :, 0:LANES]
        egl_s[hh] = jnp.broadcast_to(jnp.exp(gl3), (nc, 1, LANES))

    nw = nw_ref[...]

    def chunk_body(c, carry):
        r0 = pl.multiple_of(c * CHUNK, CHUNK)
        s_old = [st[hh] for hh in range(hg)]
        res = [jnp.dot(lhs_s[hh, c], s_old[hh].astype(BF16), preferred_element_type=F32) for hh in range(hg)]
        for hh in range(hg):
            c0 = hh * LANES
            st[hh] = egl_s[hh, c] * s_old[hh] - res[hh][0:GDN_DK, :] + bt_s[hh, c]
            o = res[hh][GDN_DK:GDN_DK + CHUNK, :] + au_s[hh, c]
            on = o * lax.rsqrt(jnp.mean(o * o, axis=-1, keepdims=True) + RMS_EPS) * nw
            zz = z_ref[pl.ds(r0, CHUNK), c0:c0 + LANES]
            o_ref[pl.ds(r0, CHUNK), c0:c0 + LANES] = (on * _silu(zz)).astype(o_ref.dtype)
        return carry

    lax.fori_loop(0, nc, chunk_body, 0)


def gdn_delta(proj, conv_w, gates, norm_w, *, batch, seq, heads, hg=4, ts=1024):
    hg = min(hg, heads)
    ts = min(ts, seq)
    nhb = heads // hg
    nst = seq // ts
    wblk = hg * LANES
    nc = ts // CHUNK
    kern = functools.partial(_gdn_kernel, hg=hg, ts=ts, heads=heads)

    def pmap(off):
        return lambda b, h, i: (b * nst + i, off * nhb + h)

    def cmap(off):
        return lambda b, h, i: (0, off * nhb + h)

    return pl.pallas_call(
        kern,
        grid=(batch, nhb, nst),
        in_specs=[pl.BlockSpec((ts, wblk), pmap(0)), pl.BlockSpec((ts, wblk), pmap(1)),
                  pl.BlockSpec((ts, wblk), pmap(2)), pl.BlockSpec((ts, wblk), pmap(3)),
                  pl.BlockSpec((4, wblk), cmap(0)), pl.BlockSpec((4, wblk), cmap(1)), pl.BlockSpec((4, wblk), cmap(2)),
                  pl.BlockSpec((ts, LANES), lambda b, h, i: (b * nst + i, 0)),
                  pl.BlockSpec((1, LANES), lambda b, h, i: (0, 0))],
        out_specs=pl.BlockSpec((ts, wblk), lambda b, h, i: (b * nst + i, h)),
        out_shape=jax.ShapeDtypeStruct((batch * seq, heads * GDN_DV), BF16),
        scratch_shapes=[pltpu.VMEM((3, 8, wblk), F32),
                        pltpu.VMEM((hg, nc, GDN_DK + CHUNK, LANES), BF16),
                        pltpu.VMEM((hg, nc, GDN_DK, GDN_DV), F32),
                        pltpu.VMEM((hg, nc, CHUNK, GDN_DV), F32),
                        pltpu.VMEM((hg, nc, 1, LANES), F32),
                        pltpu.VMEM((hg, GDN_DK, GDN_DV), F32)],
        compiler_params=_cparams(3),
        name="gdn_delta_rule",
    )(proj, proj, proj, proj, conv_w, conv_w, conv_w, gates, norm_w.reshape(1, LANES).astype(F32))


def _router_kernel(x_ref, g_ref, w_ref, b_ref, o_ref):
    xf = x_ref[...]
    var = jnp.mean(xf * xf, axis=-1, keepdims=True)
    xn = xf * lax.rsqrt(var + RMS_EPS) * g_ref[...]
    o_ref[...] = jnp.dot(xn, w_ref[...], preferred_element_type=F32, precision=lax.Precision.HIGHEST) + b_ref[...]


def router_logits(h, gain, router_w, router_b, *, tm=512):
    rows, d = h.shape
    tm = min(tm, rows)
    n_e = router_w.shape[1]
    w = jnp.pad(router_w, ((0, 0), (0, LANES - n_e)))
    b = jnp.pad(router_b.astype(F32), (0, LANES - n_e)).reshape(1, LANES)
    return pl.pallas_call(
        _router_kernel,
        grid=(rows // tm,),
        in_specs=[pl.BlockSpec((tm, d), lambda i: (i, 0)), pl.BlockSpec((1, d), lambda i: (0, 0)),
                  pl.BlockSpec((d, LANES), lambda i: (0, 0)), pl.BlockSpec((1, LANES), lambda i: (0, 0))],
        out_specs=pl.BlockSpec((tm, LANES), lambda i: (i, 0)),
        out_shape=jax.ShapeDtypeStruct((rows, LANES), F32),
        compiler_params=_cparams(1),
        name="moe_router",
    )(h, gain.reshape(1, d).astype(F32), w, b)


def _row_copy(src_hbm, dst, sem, src_row, dst_row):
    return pltpu.make_async_copy(src_hbm.at[pl.ds(src_row, 1)], dst.at[pl.ds(dst_row, 1)], sem)


def _gather_norm_kernel(tok_ref, h_hbm, g_ref, o_ref, buf, sem, *, tm, nblk):
    rb = pl.program_id(0)

    def issue(blk, slot):
        def one(r, carry):
            _row_copy(h_hbm, buf.at[slot], sem.at[slot], tok_ref[blk * tm + r], r).start()
            return carry
        lax.fori_loop(0, tm, one, 0)

    @pl.when(rb == 0)
    def _():
        issue(0, 0)

    @pl.when(rb + 1 < nblk)
    def _():
        issue(rb + 1, (rb + 1) % 2)

    slot = rb % 2
    pltpu.make_async_copy(h_hbm.at[pl.ds(0, tm)], buf.at[slot], sem.at[slot]).wait()
    xf = buf[slot]
    var = jnp.mean(xf * xf, axis=-1, keepdims=True)
    o_ref[...] = (xf * lax.rsqrt(var + RMS_EPS) * g_ref[...]).astype(o_ref.dtype)


def gather_norm(h, gain, slot_tok, *, tm):
    rows, d = h.shape
    n_slots = slot_tok.shape[0]
    nblk = n_slots // tm
    kern = functools.partial(_gather_norm_kernel, tm=tm, nblk=nblk)
    return pl.pallas_call(
        kern,
        grid_spec=pltpu.PrefetchScalarGridSpec(
            num_scalar_prefetch=1,
            grid=(nblk,),
            in_specs=[pl.BlockSpec(memory_space=pl.ANY), pl.BlockSpec((1, d), lambda i, t: (0, 0))],
            out_specs=pl.BlockSpec((tm, d), lambda i, t: (i, 0)),
            scratch_shapes=[pltpu.VMEM((2, tm, d), F32), pltpu.SemaphoreType.DMA((2,))],
        ),
        out_shape=jax.ShapeDtypeStruct((n_slots, d), BF16),
        compiler_params=_cparams(1),
        name="moe_gather_norm",
    )(slot_tok, h, gain.reshape(1, d).astype(F32))


def _combine_kernel(slot_ref, y_hbm, h_ref, gate_ref, g_ref, o_ref, buf, sem, *, tc, nblk):
    i = pl.program_id(0)

    def issue(blk, slot):
        def one(r, carry):
            for kk in range(TOP_K):
                _row_copy(y_hbm, buf.at[slot, kk], sem.at[slot], slot_ref[(blk * tc + r) * TOP_K + kk], r).start()
            return carry
        lax.fori_loop(0, tc, one, 0)

    @pl.when(i == 0)
    def _():
        issue(0, 0)

    @pl.when(i + 1 < nblk)
    def _():
        issue(i + 1, (i + 1) % 2)

    slot = i % 2
    for kk in range(TOP_K):
        pltpu.make_async_copy(y_hbm.at[pl.ds(0, tc)], buf.at[slot, kk], sem.at[slot]).wait()
    gate = gate_ref[...]
    xf = h_ref[...] + gate[:, 0:1] * buf[slot, 0] + gate[:, 1:2] * buf[slot, 1]
    var = jnp.mean(xf * xf, axis=-1, keepdims=True)
    o_ref[...] = xf * lax.rsqrt(var + RMS_EPS) * g_ref[...]


def combine_norm(h, y, tok_slots, gates, final_gain, *, tc=256):
    rows, d = h.shape
    tc = min(tc, rows)
    nblk = rows // tc
    kern = functools.partial(_combine_kernel, tc=tc, nblk=nblk)
    return pl.pallas_call(
        kern,
        grid_spec=pltpu.PrefetchScalarGridSpec(
            num_scalar_prefetch=1,
            grid=(nblk,),
            in_specs=[pl.BlockSpec(memory_space=pl.ANY), pl.BlockSpec((tc, d), lambda i, s: (i, 0)),
                      pl.BlockSpec((tc, TOP_K), lambda i, s: (i, 0)), pl.BlockSpec((1, d), lambda i, s: (0, 0))],
            out_specs=pl.BlockSpec((tc, d), lambda i, s: (i, 0)),
            scratch_shapes=[pltpu.VMEM((2, TOP_K, tc, d), F32), pltpu.SemaphoreType.DMA((2,))],
        ),
        out_shape=jax.ShapeDtypeStruct((rows, d), F32),
        compiler_params=_cparams(1),
        name="moe_combine_norm",
    )(tok_slots, y, h, gates, final_gain.reshape(1, d).astype(F32))


def _route(logits, n_experts, tm):
    n_tok = logits.shape[0]
    n_asg = n_tok * TOP_K
    nblk = n_asg // tm + n_experts
    top_logit, top_idx = lax.top_k(logits[:, :n_experts], TOP_K)
    gates = jax.nn.softmax(top_logit, axis=-1)
    flat_e = top_idx.reshape(n_asg).astype(jnp.int32)
    onehot = (flat_e[:, None] == jnp.arange(n_experts, dtype=jnp.int32)[None, :]).astype(jnp.int32)
    csum = jnp.cumsum(onehot, axis=0)
    rank = jnp.sum(csum * onehot, axis=1) - 1
    counts = csum[-1]
    padded = (counts + tm - 1) // tm * tm
    pad_end = jnp.cumsum(padded)
    pad_start = pad_end - padded
    dest = (pad_start[flat_e] + rank).astype(jnp.int32)
    slot_tok = jnp.zeros((nblk * tm,), jnp.int32).at[dest].set(jnp.arange(n_asg, dtype=jnp.int32) // TOP_K)
    nused = (pad_end[-1] // tm).astype(jnp.int32)
    blk = jnp.arange(nblk, dtype=jnp.int32)
    be = jnp.minimum(jnp.searchsorted(pad_end, blk * tm, side="right"), n_experts - 1).astype(jnp.int32)
    be = jnp.where(blk < nused, be, be[jnp.maximum(nused - 1, 0)])
    return slot_tok, gates, dest, be, nused.reshape(1)


def _rope_table(positions):
    inv_freq = ROPE_THETA ** (-jnp.arange(0, QK_ROPE, 2, dtype=F32) / QK_ROPE)
    ang = positions.astype(F32)[..., None] * inv_freq
    cos, sin = jnp.cos(ang), jnp.sin(ang)
    return jnp.concatenate([cos, cos, sin, sin], axis=-1).reshape(-1, 2 * QK_ROPE)


def _rot_cols(w):
    half = w.shape[-1] // 2
    return jnp.concatenate([-w[..., half:], w[..., :half]], axis=-1)


def _mla_layer(h, cs, ln, w_in, q_norm, w_qb, kv_norm, w_kvb, w_o, *, batch, seq):
    d = h.shape[1]
    q_lora = q_norm.shape[0]
    kv_lora = kv_norm.shape[0]
    heads = w_o.shape[0] // V_HEAD
    assert q_lora == kv_lora and q_lora % LANES == 0
    w_rope = w_in[:, q_lora + kv_lora:]
    w_in_p = jnp.concatenate([w_in, _rot_cols(w_rope)], axis=1)
    n_in = w_in_p.shape[1]
    wq = w_qb.reshape(q_lora, heads, QK_NOPE + QK_ROPE)
    wq_p = jnp.concatenate([wq, _rot_cols(wq[..., QK_NOPE:])], axis=-1).reshape(q_lora, heads * Q_HEAD_PAD)

    hn = rmsnorm(h, ln)
    proj = gmm(hn, [w_in_p[None]], n_out=n_in, tm=512, tn=n_in, out_dtype=F32, name="mla_in_proj")
    cqn = rmsnorm(proj, q_norm, col_blk=0)
    ckvn = rmsnorm(proj, kv_norm, col_blk=1)
    scale = (QK_NOPE + QK_ROPE) ** -0.5
    q = gmm(cqn, [wq_p[None]], n_out=heads * Q_HEAD_PAD, tm=512, tn=1024, out_dtype=BF16, mode="rope", cs=cs,
            scale=scale, name="mla_q_proj")
    kv = gmm(ckvn, [w_kvb[None]], n_out=heads * (QK_NOPE + V_HEAD), tm=512, tn=1024, out_dtype=BF16,
             name="mla_kv_proj")
    kpe = rope_shared_key(proj, cs, (q_lora + kv_lora) // LANES)
    o = mla_attention_core(q, kv, kpe, batch=batch, seq=seq, heads=heads)
    return gmm(o, [w_o[None]], n_out=d, tm=512, tn=1024, out_dtype=F32, residual=h, name="mla_out_proj")


def _ffn_layer(h, ln, w_gate, w_up, w_down):
    d = h.shape[1]
    f = w_gate.shape[1]
    hn = rmsnorm(h, ln)
    mid = gmm(hn, [w_gate[None], w_up[None]], n_out=f, tm=512, tn=1024 if f % 1024 == 0 else f // 7,
              out_dtype=BF16, mode="swiglu", name="ffn_up")
    return gmm(mid, [w_down[None]], n_out=d, tm=512, tn=512, out_dtype=F32, residual=h, name="ffn_down")


def _gdn_layer(h, ln, w_in, conv_w, a_log, dt_bias, norm_w, w_o, *, batch, seq):
    d = h.shape[1]
    heads = a_log.shape[0]
    n_main = 4 * heads * LANES
    hn = rmsnorm(h, ln)
    proj = gmm(hn, [w_in[None]], n_out=n_main, tm=512, tn=1024, out_dtype=F32, name="gdn_in_proj")
    gates = gdn_gates(hn, w_in[:, n_main:], a_log, dt_bias, heads=heads)
    o = gdn_delta(proj, conv_w, gates, norm_w, batch=batch, seq=seq, heads=heads)
    return gmm(o, [w_o[None]], n_out=d, tm=512, tn=1024, out_dtype=F32, residual=h, name="gdn_out_proj")


def _moe_layer_final(h, ln, router_w, router_b, w_gate, w_up, w_down, final_gain, *, tm=512):
    rows, d = h.shape
    n_e, _, f = w_gate.shape
    tm = min(tm, rows // 4)
    logits = router_logits(h, ln, router_w, router_b)
    slot_tok, gates, dest, be, nused = _route(logits, n_e, tm)
    xs = gather_norm(h, ln, slot_tok, tm=tm)
    mid = gmm(xs, [w_gate, w_up], n_out=f, tm=tm, tn=1024 if f % 1024 == 0 else f // 7, out_dtype=BF16,
              be=be, nused=nused, mode="swiglu", name="moe_up")
    y = gmm(mid, [w_down], n_out=d, tm=tm, tn=512, out_dtype=F32, be=be, nused=nused, name="moe_down")
    return combine_norm(h, y, dest, gates, final_gain)


def kernel(x, positions, ln_mix_mla, mla_w_in, mla_q_norm, mla_w_qb, mla_kv_norm, mla_w_kvb, mla_w_o, ln_ffn_dense, ffn_w_gate, ffn_w_up, ffn_w_down, ln_mix_gdn, gdn_w_in, gdn_conv_w, gdn_a_log, gdn_dt_bias, gdn_norm, gdn_w_o, ln_ffn_moe, moe_router, moe_router_bias, moe_w_gate, moe_w_up, moe_w_down, final_norm):
    batch, seq, d = x.shape
    assert ln_mix_mla.shape[0] == 1 and ln_mix_gdn.shape[0] == 1, "two-layer trunk: one MLA and one DeltaNet layer"
    h = x.reshape(batch * seq, d)
    cs = _rope_table(positions)
    h = _mla_layer(h, cs, ln_mix_mla[0], mla_w_in[0], mla_q_norm[0], mla_w_qb[0], mla_kv_norm[0], mla_w_kvb[0],
                   mla_w_o[0], batch=batch, seq=seq)
    h = _ffn_layer(h, ln_ffn_dense[0], ffn_w_gate[0], ffn_w_up[0], ffn_w_down[0])
    h = _gdn_layer(h, ln_mix_gdn[0], gdn_w_in[0], gdn_conv_w[0], gdn_a_log[0], gdn_dt_bias[0], gdn_norm[0],
                   gdn_w_o[0], batch=batch, seq=seq)
    out = _moe_layer_final(h, ln_ffn_moe[0], moe_router[0], moe_router_bias[0], moe_w_gate[0], moe_w_up[0],
                           moe_w_down[0], final_norm)
    return out.reshape(batch, seq, d)
```

```python
import functools

import jax
import jax.numpy as jnp
from jax import lax
from jax.experimental import pallas as pl
from jax.experimental.pallas import tpu as pltpu

F32 = jnp.float32
BF16 = jnp.bfloat16

RMS_EPS = 1e-6
ROPE_THETA = 10000.0
QK_NOPE = 128
QK_ROPE = 64
V_HEAD = 128
Q_HEAD_PAD = 256
GDN_DK = 128
GDN_DV = 128
CHUNK = 64
TOP_K = 2
LANES = 128
VMEM_LIMIT = 56 * 1024 * 1024
NEG_BIG = -1e30
LOG2_E = 1.4426950408889634


def _cparams(n_axes):
    return pltpu.CompilerParams(dimension_semantics=("arbitrary",) * n_axes, vmem_limit_bytes=VMEM_LIMIT)


def _silu(x):
    return x * jax.nn.sigmoid(x)


def _rmsnorm_kernel(x_ref, g_ref, o_ref):
    xf = x_ref[...].astype(F32)
    var = jnp.mean(xf * xf, axis=-1, keepdims=True)
    o_ref[...] = (xf * lax.rsqrt(var + RMS_EPS) * g_ref[...]).astype(o_ref.dtype)


def rmsnorm(x, gain, *, col_blk=0, tm=512, out_dtype=BF16):
    rows = x.shape[0]
    k = gain.shape[-1]
    tm = min(tm, rows)
    return pl.pallas_call(
        _rmsnorm_kernel,
        grid=(rows // tm,),
        in_specs=[pl.BlockSpec((tm, k), lambda i: (i, col_blk)), pl.BlockSpec((1, k), lambda i: (0, 0))],
        out_specs=pl.BlockSpec((tm, k), lambda i: (i, 0)),
        out_shape=jax.ShapeDtypeStruct((rows, k), out_dtype),
        compiler_params=_cparams(1),
        name="rmsnorm",
    )(x, gain.reshape(1, k).astype(F32))


def _gmm_kernel(be_ref, nu_ref, x_ref, *rest, n_w, mode, has_gain, has_res, scale):
    gain_ref = res_ref = cs_ref = None
    if has_gain:
        gain_ref, rest = rest[0], rest[1:]
    w_refs = rest[:n_w]
    rest = rest[n_w:]
    if has_res:
        res_ref, rest = rest[0], rest[1:]
    if mode == "rope":
        cs_ref, rest = rest[0], rest[1:]
    o_ref = rest[0]
    wbf_refs = rest[1:]
    rb = pl.program_id(1)

    prev = be_ref[jnp.maximum(rb - 1, 0)]
    changed = jnp.logical_or(rb == 0, be_ref[rb] != prev)

    @pl.when(changed)
    def _():
        for w_ref, wbf_ref in zip(w_refs, wbf_refs):
            wbf_ref[...] = w_ref[0].astype(BF16)

    @pl.when(rb < nu_ref[0])
    def _():
        x = x_ref[...]
        if has_gain:
            var = jnp.mean(x * x, axis=-1, keepdims=True)
            x = (x * lax.rsqrt(var + RMS_EPS) * gain_ref[...]).astype(BF16)
        if mode == "swiglu":
            a = jnp.dot(x, wbf_refs[0][...], preferred_element_type=F32)
            b = jnp.dot(x, wbf_refs[1][...], preferred_element_type=F32)
            o_ref[...] = (_silu(a) * b).astype(o_ref.dtype)
        else:
            acc = jnp.dot(x, wbf_refs[0][...], preferred_element_type=F32)
            if has_res:
                acc = acc + res_ref[...]
            if mode == "rope":
                cs = cs_ref[...]
                for hh in range(acc.shape[1] // Q_HEAD_PAD):
                    c0 = hh * Q_HEAD_PAD
                    o_ref[:, c0:c0 + QK_NOPE] = (acc[:, c0:c0 + QK_NOPE] * scale).astype(o_ref.dtype)
                    g = acc[:, c0 + QK_NOPE:c0 + Q_HEAD_PAD] * cs
                    r = g + pltpu.roll(g, QK_ROPE, axis=1)
                    o_ref[:, c0 + QK_NOPE:c0 + Q_HEAD_PAD] = (r * scale).astype(o_ref.dtype)
            else:
                o_ref[...] = acc.astype(o_ref.dtype)

    @pl.when(rb >= nu_ref[0])
    def _():
        o_ref[...] = jnp.zeros(o_ref.shape, o_ref.dtype)


def gmm(x, ws, *, n_out, tm, tn, out_dtype, be=None, nused=None, mode="plain", gain=None, x_col_blk=0,
        residual=None, cs=None, scale=None, name="gmm"):
    rows = x.shape[0]
    k = ws[0].shape[1]
    tm = min(tm, rows)
    tn = min(tn, n_out)
    nblk = rows // tm
    if be is None:
        be = jnp.zeros((nblk,), jnp.int32)
        nused = jnp.full((1,), nblk, jnp.int32)
    n_w = len(ws)

    def x_map(n, rb, be_r, nu_r):
        return (jnp.minimum(rb, nu_r[0] - 1), x_col_blk)

    def w_map(n, rb, be_r, nu_r):
        return (be_r[rb], 0, n)

    def row_map(n, rb, be_r, nu_r):
        return (rb, 0)

    def out_map(n, rb, be_r, nu_r):
        return (rb, n)

    in_specs = [pl.BlockSpec((tm, k), x_map)]
    args = [x]
    if gain is not None:
        in_specs.append(pl.BlockSpec((1, k), lambda n, rb, be_r, nu_r: (0, 0)))
        args.append(gain.reshape(1, k).astype(F32))
    in_specs += [pl.BlockSpec((1, k, tn), w_map) for _ in ws]
    args += list(ws)
    if residual is not None:
        in_specs.append(pl.BlockSpec((tm, tn), out_map))
        args.append(residual)
    if mode == "rope":
        in_specs.append(pl.BlockSpec((tm, LANES), row_map))
        args.append(cs)
    kern = functools.partial(_gmm_kernel, n_w=n_w, mode=mode, has_gain=gain is not None,
                             has_res=residual is not None, scale=scale)
    return pl.pallas_call(
        kern,
        grid_spec=pltpu.PrefetchScalarGridSpec(
            num_scalar_prefetch=2,
            grid=(n_out // tn, nblk),
            in_specs=in_specs,
            out_specs=pl.BlockSpec((tm, tn), out_map),
            scratch_shapes=[pltpu.VMEM((k, tn), BF16) for _ in ws],
        ),
        out_shape=jax.ShapeDtypeStruct((rows, n_out), out_dtype),
        compiler_params=_cparams(2),
        name=name,
    )(be, nused, *args)


def _kpe_kernel(x_ref, cs_ref, o_ref):
    g = x_ref[...] * cs_ref[...]
    r = g + pltpu.roll(g, QK_ROPE, axis=1)
    lane = lax.broadcasted_iota(jnp.int32, r.shape, 1)
    o_ref[...] = jnp.where(lane < QK_ROPE, r, 0.0).astype(o_ref.dtype)


def rope_shared_key(proj, cs, col_blk, *, tm=512):
    rows = proj.shape[0]
    tm = min(tm, rows)
    return pl.pallas_call(
        _kpe_kernel,
        grid=(rows // tm,),
        in_specs=[pl.BlockSpec((tm, LANES), lambda i: (i, col_blk)), pl.BlockSpec((tm, LANES), lambda i: (i, 0))],
        out_specs=pl.BlockSpec((tm, LANES), lambda i: (i, 0)),
        out_shape=jax.ShapeDtypeStruct((rows, LANES), BF16),
        compiler_params=_cparams(1),
        name="rope_shared_key",
    )(proj, cs)


def _attn_kernel(q_ref, kv_ref, kpe_ref, o_ref, *, tq, tk, hp):
    qi = pl.program_id(2)
    n_full = (qi * tq) // tk
    n_diag = max(1, tq // tk)
    row = lax.broadcasted_iota(jnp.int32, (tq, tk), 0) + qi * tq
    col0 = lax.broadcasted_iota(jnp.int32, (tq, tk), 1)
    qs = [q_ref[:, a * Q_HEAD_PAD:(a + 1) * Q_HEAD_PAD] for a in range(hp)]

    def step(j, carry, masked):
        r0 = pl.multiple_of(j * tk, tk)
        kpe = kpe_ref[pl.ds(r0, tk), :]
        out = []
        for a in range(hp):
            m, l, acc = carry[a]
            c0 = a * (QK_NOPE + V_HEAD)
            k = jnp.concatenate([kv_ref[pl.ds(r0, tk), c0:c0 + QK_NOPE], kpe], axis=1)
            v = kv_ref[pl.ds(r0, tk), c0 + QK_NOPE:c0 + QK_NOPE + V_HEAD]
            s = lax.dot_general(qs[a], k, (((1,), (1,)), ((), ())), preferred_element_type=F32)
            if masked:
                s = jnp.where(col0 + j * tk <= row, s, NEG_BIG)
            m_new = jnp.maximum(m, jnp.max(s, axis=1, keepdims=True))
            p = jnp.exp2(s - m_new)
            alpha = jnp.exp2(m - m_new)
            l = alpha * l + jnp.sum(p, axis=1, keepdims=True)
            acc = alpha * acc + jnp.dot(p.astype(BF16), v, preferred_element_type=F32)
            out.append((m_new, l, acc))
        return tuple(out)

    init = tuple((jnp.full((tq, 1), NEG_BIG, F32), jnp.zeros((tq, 1), F32), jnp.zeros((tq, V_HEAD), F32))
                 for _ in range(hp))
    carry = lax.fori_loop(0, n_full, lambda j, c: step(j, c, False), init)
    for jd in range(n_diag):
        carry = step(n_full + jd, carry, True)
    for a in range(hp):
        _, l, acc = carry[a]
        o_ref[:, a * V_HEAD:(a + 1) * V_HEAD] = (acc / l).astype(o_ref.dtype)


def mla_attention_core(q, kv, kpe, *, batch, seq, heads, tq=512, tk=512, hp=2):
    tq = min(tq, seq)
    tk = min(tk, seq)
    hp = min(hp, heads)
    assert (tk % tq == 0 or tq % tk == 0) and seq % tk == 0
    nq = seq // tq
    kern = functools.partial(_attn_kernel, tq=tq, tk=tk, hp=hp)
    return pl.pallas_call(
        kern,
        grid=(batch, heads // hp, nq),
        in_specs=[
            pl.BlockSpec((tq, hp * Q_HEAD_PAD), lambda b, h, i: (b * nq + i, h)),
            pl.BlockSpec((seq, hp * (QK_NOPE + V_HEAD)), lambda b, h, i: (b, h)),
            pl.BlockSpec((seq, LANES), lambda b, h, i: (b, 0)),
        ],
        out_specs=pl.BlockSpec((tq, hp * V_HEAD), lambda b, h, i: (b * nq + i, h)),
        out_shape=jax.ShapeDtypeStruct((batch * seq, heads * V_HEAD), BF16),
        compiler_params=_cparams(3),
        name="mla_flash_attention",
    )(q, kv, kpe)


def _gdn_gate_kernel(x_ref, w_ref, alog_ref, dtb_ref, o_ref, *, heads):
    ba = jnp.dot(x_ref[...], w_ref[...].astype(BF16), preferred_element_type=F32)
    lane = lax.broadcasted_iota(jnp.int32, ba.shape, 1)
    beta = jax.nn.sigmoid(ba)
    z = ba + dtb_ref[...]
    softplus = jnp.maximum(z, 0.0) + jnp.log1p(jnp.exp(-jnp.abs(z)))
    g = -jnp.exp(alog_ref[...]) * softplus
    o_ref[...] = jnp.where(lane < heads, beta, g)


def gdn_gates(xn, w_ba, a_log, dt_bias, *, heads, tm=512):
    rows, k = xn.shape
    tm = min(tm, rows)
    pad = LANES - 2 * heads
    w = jnp.pad(w_ba, ((0, 0), (0, pad)))
    alog = jnp.pad(a_log.astype(F32), (heads, pad)).reshape(1, LANES)
    dtb = jnp.pad(dt_bias.astype(F32), (heads, pad)).reshape(1, LANES)
    return pl.pallas_call(
        functools.partial(_gdn_gate_kernel, heads=heads),
        grid=(rows // tm,),
        in_specs=[pl.BlockSpec((tm, k), lambda i: (i, 0)), pl.BlockSpec((k, LANES), lambda i: (0, 0)),
                  pl.BlockSpec((1, LANES), lambda i: (0, 0)), pl.BlockSpec((1, LANES), lambda i: (0, 0))],
        out_specs=pl.BlockSpec((tm, LANES), lambda i: (i, 0)),
        out_shape=jax.ShapeDtypeStruct((rows, LANES), F32),
        compiler_params=_cparams(1),
        name="gdn_gates",
    )(xn, w, alog, dtb)


def _bmm(a, b):
    return lax.dot_general(a, b, (((2,), (1,)), ((0,), (0,))), preferred_element_type=F32)


def _bmm_nt(a, b):
    return lax.dot_general(a, b, (((2,), (2,)), ((0,), (0,))), preferred_element_type=F32)


def _bmm_tn(a, b):
    return lax.dot_general(a, b, (((1,), (1,)), ((0,), (0,))), preferred_element_type=F32)


def _gdn_kernel(q_ref, k_ref, v_ref, z_ref, cwq_ref, cwk_ref, cwv_ref, gate_ref, nw_ref, o_ref,
                halo, xx, lhs_s, bt_s, au_s, egl_s, st, *, hg, ts, heads):
    nc = ts // CHUNK
    hgi = pl.program_id(1)
    first = pl.program_id(2) == 0

    lane_t = lax.broadcasted_iota(jnp.int32, (ts, LANES), 1)
    ii = lax.broadcasted_iota(jnp.int32, (CHUNK, CHUNK), 0)
    jj = lax.broadcasted_iota(jnp.int32, (CHUNK, CHUNK), 1)
    incl = ii >= jj
    strict = ii > jj
    lower_ones = jnp.where(incl, 1.0, 0.0).astype(F32)
    eye = jnp.where(ii == jj, 1.0, 0.0).astype(F32)
    gates = gate_ref[...]

    def conv_silu(idx, x_ref, w_ref, c0):
        x = x_ref[:, c0:c0 + LANES]
        w = w_ref[:, c0:c0 + LANES]
        xb = xx.at[idx * hg + c0 // LANES]
        xb[0:8, :] = jnp.where(first, 0.0, halo[idx, :, c0:c0 + LANES])
        xb[8:ts + 8, :] = x
        halo[idx, :, c0:c0 + LANES] = x[ts - 8:ts, :]
        y = x * w[3:4, :]
        for s in (1, 2, 3):
            y = y + xb[8 - s:8 - s + ts, :] * w[3 - s:4 - s, :]
        return _silu(y)

    @pl.when(first)
    def _():
        st[...] = jnp.zeros(st.shape, F32)

    for hh in range(hg):
        c0 = hh * LANES
        head = hgi * hg + hh
        q = conv_silu(0, q_ref, cwq_ref, c0)
        q = q * lax.rsqrt(jnp.sum(q * q, axis=-1, keepdims=True) + RMS_EPS) * (GDN_DK ** -0.5)
        k = conv_silu(1, k_ref, cwk_ref, c0)
        k = k * lax.rsqrt(jnp.sum(k * k, axis=-1, keepdims=True) + RMS_EPS)
        v = conv_silu(2, v_ref, cwv_ref, c0)

        b_col = jnp.sum(jnp.where(lane_t == head, gates, 0.0), axis=1, keepdims=True)
        g_col = jnp.sum(jnp.where(lane_t == heads + head, gates, 0.0), axis=1, keepdims=True)
        b3 = b_col.reshape(nc, CHUNK, 1)
        g3 = g_col.reshape(nc, CHUNK, 1)
        g_row3 = jnp.sum(g3 * eye, axis=1, keepdims=True)
        gc3 = jnp.sum(lower_ones * g_row3, axis=2, keepdims=True)
        gc_row3 = jnp.sum(gc3 * eye, axis=1, keepdims=True)
        gl3 = gc3[:, CHUNK - 1:CHUNK, :]
        decay = jnp.where(incl, jnp.exp(jnp.where(incl, gc3 - gc_row3, 0.0)), 0.0)
        egc3 = jnp.exp(gc3)

        q3 = q.reshape(nc, CHUNK, LANES)
        k3 = k.reshape(nc, CHUNK, LANES)
        v3 = v.reshape(nc, CHUNK, LANES)
        q16 = q3.astype(BF16)
        k16 = k3.astype(BF16)
        lower = jnp.where(strict, _bmm_nt(k16, k16) * decay * b3, 0.0)
        p = -lower
        t_inv = eye + p
        for _ in range(5):
            p16 = p.astype(BF16)
            p = _bmm(p16, p16)
            t_inv = t_inv + _bmm(t_inv.astype(BF16), p.astype(BF16))
        vk16 = jnp.concatenate([(v3 * b3).astype(BF16), (k3 * (b3 * egc3)).astype(BF16)], axis=-1)
        uw16 = _bmm(t_inv.astype(BF16), vk16).astype(BF16)
        a16 = jnp.where(incl, _bmm_nt(q16, k16) * decay, 0.0).astype(BF16)
        awu = _bmm(a16, uw16)
        kd16 = (k3 * jnp.exp(gl3 - gc3)).astype(BF16)
        mb = _bmm_tn(kd16, uw16)
        lhs_s[hh, :, 0:GDN_DK, :] = mb[:, :, LANES:].astype(BF16)
        lhs_s[hh, :, GDN_DK:GDN_DK + CHUNK, :] = (q3 * egc3 - awu[:, :, LANES:]).astype(BF16)
        bt_s[hh] = mb[:, :, 0:LANES]
        au_s[hh] = awu[:, :, 0:LANES]
        egl_s[hh] = jnp.broadcast_to(jnp.exp(gl3), (nc, 1, LANES))

    def chunk_body(c, carry):
        s_old = [st[hh] for hh in range(hg)]
        res = [jnp.dot(lhs_s[hh, c], s_old[hh].astype(BF16), preferred_element_type=F32) for hh in range(hg)]
        for hh in range(hg):
            st[hh] = egl_s[hh, c] * s_old[hh] - res[hh][0:GDN_DK, :] + bt_s[hh, c]
            au_s[hh, c] = res[hh][GDN_DK:GDN_DK + CHUNK, :] + au_s[hh, c]
        return carry

    lax.fori_loop(0, nc, chunk_body, 0)

    nw = nw_ref[...]
    for hh in range(hg):
        c0 = hh * LANES
        o = au_s[hh].reshape(ts, LANES)
        on = o * lax.rsqrt(jnp.mean(o * o, axis=-1, keepdims=True) + RMS_EPS) * nw
        o_ref[:, c0:c0 + LANES] = (on * _silu(z_ref[:, c0:c0 + LANES])).astype(o_ref.dtype)


def gdn_delta(proj, conv_w, gates, norm_w, *, batch, seq, heads, hg=4, ts=1024):
    hg = min(hg, heads)
    ts = min(ts, seq)
    nhb = heads // hg
    nst = seq // ts
    wblk = hg * LANES
    nc = ts // CHUNK
    kern = functools.partial(_gdn_kernel, hg=hg, ts=ts, heads=heads)

    def pmap(off):
        return lambda b, h, i: (b * nst + i, off * nhb + h)

    def cmap(off):
        return lambda b, h, i: (0, off * nhb + h)

    return pl.pallas_call(
        kern,
        grid=(batch, nhb, nst),
        in_specs=[pl.BlockSpec((ts, wblk), pmap(0)), pl.BlockSpec((ts, wblk), pmap(1)),
                  pl.BlockSpec((ts, wblk), pmap(2)), pl.BlockSpec((ts, wblk), pmap(3)),
                  pl.BlockSpec((4, wblk), cmap(0)), pl.BlockSpec((4, wblk), cmap(1)), pl.BlockSpec((4, wblk), cmap(2)),
                  pl.BlockSpec((ts, LANES), lambda b, h, i: (b * nst + i, 0)),
                  pl.BlockSpec((1, LANES), lambda b, h, i: (0, 0))],
        out_specs=pl.BlockSpec((ts, wblk), lambda b, h, i: (b * nst + i, h)),
        out_shape=jax.ShapeDtypeStruct((batch * seq, heads * GDN_DV), BF16),
        scratch_shapes=[pltpu.VMEM((3, 8, wblk), F32),
                        pltpu.VMEM((3 * hg, ts + 8, LANES), F32),
                        pltpu.VMEM((hg, nc, GDN_DK + CHUNK, LANES), BF16),
                        pltpu.VMEM((hg, nc, GDN_DK, GDN_DV), F32),
                        pltpu.VMEM((hg, nc, CHUNK, GDN_DV), F32),
                        pltpu.VMEM((hg, nc, 1, LANES), F32),
                        pltpu.VMEM((hg, GDN_DK, GDN_DV), F32)],
        compiler_params=_cparams(3),
        name="gdn_delta_rule",
    )(proj, proj, proj, proj, conv_w, conv_w, conv_w, gates, norm_w.reshape(1, LANES).astype(F32))


def _router_kernel(x_ref, g_ref, w_ref, b_ref, o_ref):
    xf = x_ref[...]
    var = jnp.mean(xf * xf, axis=-1, keepdims=True)
    xn = xf * lax.rsqrt(var + RMS_EPS) * g_ref[...]
    o_ref[...] = jnp.dot(xn, w_ref[...], preferred_element_type=F32, precision=lax.Precision.HIGHEST) + b_ref[...]


def router_logits(h, gain, router_w, router_b, *, tm=512):
    rows, d = h.shape
    tm = min(tm, rows)
    n_e = router_w.shape[1]
    w = jnp.pad(router_w, ((0, 0), (0, LANES - n_e)))
    b = jnp.pad(router_b.astype(F32), (0, LANES - n_e)).reshape(1, LANES)
    return pl.pallas_call(
        _router_kernel,
        grid=(rows // tm,),
        in_specs=[pl.BlockSpec((tm, d), lambda i: (i, 0)), pl.BlockSpec((1, d), lambda i: (0, 0)),
                  pl.BlockSpec((d, LANES), lambda i: (0, 0)), pl.BlockSpec((1, LANES), lambda i: (0, 0))],
        out_specs=pl.BlockSpec((tm, LANES), lambda i: (i, 0)),
        out_shape=jax.ShapeDtypeStruct((rows, LANES), F32),
        compiler_params=_cparams(1),
        name="moe_router",
    )(h, gain.reshape(1, d).astype(F32), w, b)


def _row_copy(src_hbm, dst, sem, src_row, dst_row):
    return pltpu.make_async_copy(src_hbm.at[pl.ds(src_row, 1)], dst.at[pl.ds(dst_row, 1)], sem)


def _gather_norm_kernel(tok_ref, h_hbm, g_ref, o_ref, buf, sem, *, tm, nblk):
    rb = pl.program_id(0)

    def issue(blk, slot):
        def one(r, carry):
            _row_copy(h_hbm, buf.at[slot], sem.at[slot], tok_ref[blk * tm + r], r).start()
            return carry
        lax.fori_loop(0, tm, one, 0, unroll=8)

    @pl.when(rb == 0)
    def _():
        issue(0, 0)

    @pl.when(rb + 1 < nblk)
    def _():
        issue(rb + 1, (rb + 1) % 2)

    slot = rb % 2
    pltpu.make_async_copy(h_hbm.at[pl.ds(0, tm)], buf.at[slot], sem.at[slot]).wait()
    xf = buf[slot]
    var = jnp.mean(xf * xf, axis=-1, keepdims=True)
    o_ref[...] = (xf * lax.rsqrt(var + RMS_EPS) * g_ref[...]).astype(o_ref.dtype)


def gather_norm(h, gain, slot_tok, *, tm):
    rows, d = h.shape
    n_slots = slot_tok.shape[0]
    nblk = n_slots // tm
    kern = functools.partial(_gather_norm_kernel, tm=tm, nblk=nblk)
    return pl.pallas_call(
        kern,
        grid_spec=pltpu.PrefetchScalarGridSpec(
            num_scalar_prefetch=1,
            grid=(nblk,),
            in_specs=[pl.BlockSpec(memory_space=pl.ANY), pl.BlockSpec((1, d), lambda i, t: (0, 0))],
            out_specs=pl.BlockSpec((tm, d), lambda i, t: (i, 0)),
            scratch_shapes=[pltpu.VMEM((2, tm, d), F32), pltpu.SemaphoreType.DMA((2,))],
        ),
        out_shape=jax.ShapeDtypeStruct((n_slots, d), BF16),
        compiler_params=_cparams(1),
        name="moe_gather_norm",
    )(slot_tok, h, gain.reshape(1, d).astype(F32))


def _combine_kernel(slot_ref, y_hbm, h_ref, gate_ref, g_ref, o_ref, buf, sem, *, tc, nblk):
    i = pl.program_id(0)

    def issue(blk, slot):
        def one(r, carry):
            for kk in range(TOP_K):
                _row_copy(y_hbm, buf.at[slot, kk], sem.at[slot], slot_ref[(blk * tc + r) * TOP_K + kk], r).start()
            return carry
        lax.fori_loop(0, tc, one, 0, unroll=8)

    @pl.when(i == 0)
    def _():
        issue(0, 0)

    @pl.when(i + 1 < nblk)
    def _():
        issue(i + 1, (i + 1) % 2)

    slot = i % 2
    for kk in range(TOP_K):
        pltpu.make_async_copy(y_hbm.at[pl.ds(0, tc)], buf.at[slot, kk], sem.at[slot]).wait()
    gate = gate_ref[...]
    xf = h_ref[...] + gate[:, 0:1] * buf[slot, 0] + gate[:, 1:2] * buf[slot, 1]
    var = jnp.mean(xf * xf, axis=-1, keepdims=True)
    o_ref[...] = xf * lax.rsqrt(var + RMS_EPS) * g_ref[...]


def combine_norm(h, y, tok_slots, gates, final_gain, *, tc=256):
    rows, d = h.shape
    tc = min(tc, rows)
    nblk = rows // tc
    kern = functools.partial(_combine_kernel, tc=tc, nblk=nblk)
    return pl.pallas_call(
        kern,
        grid_spec=pltpu.PrefetchScalarGridSpec(
            num_scalar_prefetch=1,
            grid=(nblk,),
            in_specs=[pl.BlockSpec(memory_space=pl.ANY), pl.BlockSpec((tc, d), lambda i, s: (i, 0)),
                      pl.BlockSpec((tc, TOP_K), lambda i, s: (i, 0)), pl.BlockSpec((1, d), lambda i, s: (0, 0))],
            out_specs=pl.BlockSpec((tc, d), lambda i, s: (i, 0)),
            scratch_shapes=[pltpu.VMEM((2, TOP_K, tc, d), F32), pltpu.SemaphoreType.DMA((2,))],
        ),
        out_shape=jax.ShapeDtypeStruct((rows, d), F32),
        compiler_params=_cparams(1),
        name="moe_combine_norm",
    )(tok_slots, y, h, gates, final_gain.reshape(1, d).astype(F32))


def _route(logits, n_experts, tm):
    n_tok = logits.shape[0]
    n_asg = n_tok * TOP_K
    nblk = n_asg // tm + n_experts
    top_logit, top_idx = lax.top_k(logits[:, :n_experts], TOP_K)
    gates = jax.nn.softmax(top_logit, axis=-1)
    flat_e = top_idx.reshape(n_asg).astype(jnp.int32)
    onehot = (flat_e[:, None] == jnp.arange(n_experts, dtype=jnp.int32)[None, :]).astype(jnp.int32)
    csum = jnp.cumsum(onehot, axis=0)
    rank = jnp.sum(csum * onehot, axis=1) - 1
    counts = csum[-1]
    padded = (counts + tm - 1) // tm * tm
    pad_end = jnp.cumsum(padded)
    pad_start = pad_end - padded
    dest = (pad_start[flat_e] + rank).astype(jnp.int32)
    slot_tok = jnp.zeros((nblk * tm,), jnp.int32).at[dest].set(jnp.arange(n_asg, dtype=jnp.int32) // TOP_K)
    nused = (pad_end[-1] // tm).astype(jnp.int32)
    blk = jnp.arange(nblk, dtype=jnp.int32)
    be = jnp.minimum(jnp.searchsorted(pad_end, blk * tm, side="right"), n_experts - 1).astype(jnp.int32)
    be = jnp.where(blk < nused, be, be[jnp.maximum(nused - 1, 0)])
    return slot_tok, gates, dest, be, nused.reshape(1)


def _rope_table(positions):
    inv_freq = ROPE_THETA ** (-jnp.arange(0, QK_ROPE, 2, dtype=F32) / QK_ROPE)
    ang = positions.astype(F32)[..., None] * inv_freq
    cos, sin = jnp.cos(ang), jnp.sin(ang)
    return jnp.concatenate([cos, cos, sin, sin], axis=-1).reshape(-1, 2 * QK_ROPE)


def _rot_cols(w):
    half = w.shape[-1] // 2
    return jnp.concatenate([-w[..., half:], w[..., :half]], axis=-1)


def _mla_layer(h, cs, ln, w_in, q_norm, w_qb, kv_norm, w_kvb, w_o, *, batch, seq):
    d = h.shape[1]
    q_lora = q_norm.shape[0]
    kv_lora = kv_norm.shape[0]
    heads = w_o.shape[0] // V_HEAD
    assert q_lora == kv_lora and q_lora % LANES == 0
    w_rope = w_in[:, q_lora + kv_lora:]
    w_in_p = jnp.concatenate([w_in, _rot_cols(w_rope)], axis=1)
    n_in = w_in_p.shape[1]
    wq = w_qb.reshape(q_lora, heads, QK_NOPE + QK_ROPE)
    wq_p = jnp.concatenate([wq, _rot_cols(wq[..., QK_NOPE:])], axis=-1).reshape(q_lora, heads * Q_HEAD_PAD)

    proj = gmm(h, [w_in_p[None]], n_out=n_in, tm=512, tn=n_in, out_dtype=F32, gain=ln, name="mla_in_proj")
    scale = LOG2_E * (QK_NOPE + QK_ROPE) ** -0.5
    q = gmm(proj, [wq_p[None]], n_out=heads * Q_HEAD_PAD, tm=2048, tn=1024, out_dtype=BF16, mode="rope", cs=cs,
            scale=scale, gain=q_norm, x_col_blk=0, name="mla_q_proj")
    kv = gmm(proj, [w_kvb[None]], n_out=heads * (QK_NOPE + V_HEAD), tm=2048, tn=1024, out_dtype=BF16,
             gain=kv_norm, x_col_blk=1, name="mla_kv_proj")
    kpe = rope_shared_key(proj, cs, (q_lora + kv_lora) // LANES)
    o = mla_attention_core(q, kv, kpe, batch=batch, seq=seq, heads=heads)
    return gmm(o, [w_o[None]], n_out=d, tm=1024, tn=1024, out_dtype=F32, residual=h, name="mla_out_proj")


def _ffn_layer(h, ln, w_gate, w_up, w_down):
    d = h.shape[1]
    f = w_gate.shape[1]
    hn = rmsnorm(h, ln)
    mid = gmm(hn, [w_gate[None], w_up[None]], n_out=f, tm=512, tn=1024 if f % 1024 == 0 else f // 7,
              out_dtype=BF16, mode="swiglu", name="ffn_up")
    return gmm(mid, [w_down[None]], n_out=d, tm=512, tn=512, out_dtype=F32, residual=h, name="ffn_down")


def _gdn_layer(h, ln, w_in, conv_w, a_log, dt_bias, norm_w, w_o, *, batch, seq):
    d = h.shape[1]
    heads = a_log.shape[0]
    n_main = 4 * heads * LANES
    hn = rmsnorm(h, ln)
    proj = gmm(hn, [w_in[None]], n_out=n_main, tm=512, tn=1024, out_dtype=F32, name="gdn_in_proj")
    gates = gdn_gates(hn, w_in[:, n_main:], a_log, dt_bias, heads=heads)
    o = gdn_delta(proj, conv_w, gates, norm_w, batch=batch, seq=seq, heads=heads)
    return gmm(o, [w_o[None]], n_out=d, tm=1024, tn=1024, out_dtype=F32, residual=h, name="gdn_out_proj")


def _moe_layer_final(h, ln, router_w, router_b, w_gate, w_up, w_down, final_gain, *, tm=512):
    rows, d = h.shape
    n_e, _, f = w_gate.shape
    tm = min(tm, rows // 4)
    logits = router_logits(h, ln, router_w, router_b)
    slot_tok, gates, dest, be, nused = _route(logits, n_e, tm)
    xs = gather_norm(h, ln, slot_tok, tm=tm)
    mid = gmm(xs, [w_gate, w_up], n_out=f, tm=tm, tn=1024 if f % 1024 == 0 else f // 7, out_dtype=BF16,
              be=be, nused=nused, mode="swiglu", name="moe_up")
    y = gmm(mid, [w_down], n_out=d, tm=tm, tn=512, out_dtype=F32, be=be, nused=nused, name="moe_down")
    return combine_norm(h, y, dest, gates, final_gain)


def kernel(x, positions, ln_mix_mla, mla_w_in, mla_q_norm, mla_w_qb, mla_kv_norm, mla_w_kvb, mla_w_o, ln_ffn_dense, ffn_w_gate, ffn_w_up, ffn_w_down, ln_mix_gdn, gdn_w_in, gdn_conv_w, gdn_a_log, gdn_dt_bias, gdn_norm, gdn_w_o, ln_ffn_moe, moe_router, moe_router_bias, moe_w_gate, moe_w_up, moe_w_down, final_norm):
    batch, seq, d = x.shape
    assert ln_mix_mla.shape[0] == 1 and ln_mix_gdn.shape[0] == 1, "two-layer trunk: one MLA and one DeltaNet layer"
    h = x.reshape(batch * seq, d)
    cs = _rope_table(positions)
    h = _mla_layer(h, cs, ln_mix_mla[0], mla_w_in[0], mla_q_norm[0], mla_w_qb[0], mla_kv_norm[0], mla_w_kvb[0],
                   mla_w_o[0], batch=batch, seq=seq)
    h = _ffn_layer(h, ln_ffn_dense[0], ffn_w_gate[0], ffn_w_up[0], ffn_w_down[0])
    h = _gdn_layer(h, ln_mix_gdn[0], gdn_w_in[0], gdn_conv_w[0], gdn_a_log[0], gdn_dt_bias[0], gdn_norm[0],
                   gdn_w_o[0], batch=batch, seq=seq)
    out = _moe_layer_final(h, ln_ffn_moe[0], moe_router[0], moe_router_bias[0], moe_w_gate[0], moe_w_up[0],
                           moe_w_down[0], final_norm)
    return out.reshape(batch, seq, d)
```

```python
import functools

import jax
import jax.numpy as jnp
from jax import lax
from jax.experimental import pallas as pl
from jax.experimental.pallas import tpu as pltpu

F32 = jnp.float32
BF16 = jnp.bfloat16

RMS_EPS = 1e-6
ROPE_THETA = 10000.0
QK_NOPE = 128
QK_ROPE = 64
V_HEAD = 128
Q_HEAD_PAD = 256
GDN_DK = 128
GDN_DV = 128
CHUNK = 64
TOP_K = 2
LANES = 128
VMEM_LIMIT = 56 * 1024 * 1024
NEG_BIG = -1e30
LOG2_E = 1.4426950408889634


def _cparams(n_axes):
    return pltpu.CompilerParams(dimension_semantics=("arbitrary",) * n_axes, vmem_limit_bytes=VMEM_LIMIT)


def _silu(x):
    return x * jax.nn.sigmoid(x)


def _rmsnorm_kernel(x_ref, g_ref, o_ref):
    xf = x_ref[...].astype(F32)
    var = jnp.mean(xf * xf, axis=-1, keepdims=True)
    o_ref[...] = (xf * lax.rsqrt(var + RMS_EPS) * g_ref[...]).astype(o_ref.dtype)


def rmsnorm(x, gain, *, col_blk=0, tm=512, out_dtype=BF16):
    rows = x.shape[0]
    k = gain.shape[-1]
    tm = min(tm, rows)
    return pl.pallas_call(
        _rmsnorm_kernel,
        grid=(rows // tm,),
        in_specs=[pl.BlockSpec((tm, k), lambda i: (i, col_blk)), pl.BlockSpec((1, k), lambda i: (0, 0))],
        out_specs=pl.BlockSpec((tm, k), lambda i: (i, 0)),
        out_shape=jax.ShapeDtypeStruct((rows, k), out_dtype),
        compiler_params=_cparams(1),
        name="rmsnorm",
    )(x, gain.reshape(1, k).astype(F32))


def _gmm_kernel(be_ref, nu_ref, nxt_ref, x_ref, *rest, n_w, n_tiles, tn, mode, has_gain, has_res, scale):
    gain_ref = res_ref = cs_ref = None
    if has_gain:
        gain_ref, rest = rest[0], rest[1:]
    w_hbm = rest[:n_w]
    rest = rest[n_w:]
    if has_res:
        res_ref, rest = rest[0], rest[1:]
    if mode == "rope":
        cs_ref, rest = rest[0], rest[1:]
    o_ref, stage, wbf, sem = rest
    n = pl.program_id(0)
    rb = pl.program_id(1)

    def weight_copy(i, expert, n_tile):
        col = pl.multiple_of(n_tile * tn, tn)
        return pltpu.make_async_copy(w_hbm[i].at[expert, :, pl.ds(col, tn)], stage.at[i], sem.at[i])

    def start_weights(expert, n_tile):
        for i in range(n_w):
            weight_copy(i, expert, n_tile).start()

    prev = be_ref[jnp.maximum(rb - 1, 0)]
    changed = jnp.logical_or(rb == 0, be_ref[rb] != prev)

    @pl.when(changed)
    def _():
        @pl.when(jnp.logical_and(n == 0, rb == 0))
        def _():
            start_weights(be_ref[0], 0)

        for i in range(n_w):
            weight_copy(i, 0, 0).wait()
            wbf[i] = stage[i].astype(BF16)
        nxt = nxt_ref[rb]

        @pl.when(nxt >= 0)
        def _():
            start_weights(nxt, n)

        @pl.when(jnp.logical_and(nxt < 0, n + 1 < n_tiles))
        def _():
            start_weights(be_ref[0], n + 1)

    @pl.when(rb < nu_ref[0])
    def _():
        x = x_ref[...]
        if has_gain:
            var = jnp.mean(x * x, axis=-1, keepdims=True)
            x = (x * lax.rsqrt(var + RMS_EPS) * gain_ref[...]).astype(BF16)
        if mode == "swiglu":
            a = jnp.dot(x, wbf[0], preferred_element_type=F32)
            b = jnp.dot(x, wbf[1], preferred_element_type=F32)
            o_ref[...] = (_silu(a) * b).astype(o_ref.dtype)
        else:
            acc = jnp.dot(x, wbf[0], preferred_element_type=F32)
            if has_res:
                acc = acc + res_ref[...]
            if mode == "rope":
                cs = cs_ref[...]
                for hh in range(acc.shape[1] // Q_HEAD_PAD):
                    c0 = hh * Q_HEAD_PAD
                    o_ref[:, c0:c0 + QK_NOPE] = (acc[:, c0:c0 + QK_NOPE] * scale).astype(o_ref.dtype)
                    g = acc[:, c0 + QK_NOPE:c0 + Q_HEAD_PAD] * cs
                    r = g + pltpu.roll(g, QK_ROPE, axis=1)
                    o_ref[:, c0 + QK_NOPE:c0 + Q_HEAD_PAD] = (r * scale).astype(o_ref.dtype)
            else:
                o_ref[...] = acc.astype(o_ref.dtype)

    @pl.when(rb >= nu_ref[0])
    def _():
        o_ref[...] = jnp.zeros(o_ref.shape, o_ref.dtype)


def gmm(x, ws, *, n_out, tm, tn, out_dtype, be=None, nused=None, nxt=None, mode="plain", gain=None, x_col_blk=0,
        residual=None, cs=None, scale=None, name="gmm"):
    rows = x.shape[0]
    k = ws[0].shape[1]
    tm = min(tm, rows)
    tn = min(tn, n_out)
    nblk = rows // tm
    n_tiles = n_out // tn
    if be is None:
        be = jnp.zeros((nblk,), jnp.int32)
        nused = jnp.full((1,), nblk, jnp.int32)
        nxt = jnp.full((nblk,), -1, jnp.int32)
    n_w = len(ws)

    def x_map(n, rb, be_r, nu_r, nx_r):
        return (jnp.minimum(rb, nu_r[0] - 1), x_col_blk)

    def row_map(n, rb, be_r, nu_r, nx_r):
        return (rb, 0)

    def out_map(n, rb, be_r, nu_r, nx_r):
        return (rb, n)

    in_specs = [pl.BlockSpec((tm, k), x_map)]
    args = [x]
    if gain is not None:
        in_specs.append(pl.BlockSpec((1, k), lambda n, rb, be_r, nu_r, nx_r: (0, 0)))
        args.append(gain.reshape(1, k).astype(F32))
    in_specs += [pl.BlockSpec(memory_space=pl.ANY) for _ in ws]
    args += list(ws)
    if residual is not None:
        in_specs.append(pl.BlockSpec((tm, tn), out_map))
        args.append(residual)
    if mode == "rope":
        in_specs.append(pl.BlockSpec((tm, LANES), row_map))
        args.append(cs)
    kern = functools.partial(_gmm_kernel, n_w=n_w, n_tiles=n_tiles, tn=tn, mode=mode, has_gain=gain is not None,
                             has_res=residual is not None, scale=scale)
    return pl.pallas_call(
        kern,
        grid_spec=pltpu.PrefetchScalarGridSpec(
            num_scalar_prefetch=3,
            grid=(n_tiles, nblk),
            in_specs=in_specs,
            out_specs=pl.BlockSpec((tm, tn), out_map),
            scratch_shapes=[pltpu.VMEM((n_w, k, tn), F32), pltpu.VMEM((n_w, k, tn), BF16),
                            pltpu.SemaphoreType.DMA((n_w,))],
        ),
        out_shape=jax.ShapeDtypeStruct((rows, n_out), out_dtype),
        compiler_params=_cparams(2),
        name=name,
    )(be, nused, nxt, *args)


def _kpe_kernel(x_ref, cs_ref, o_ref):
    g = x_ref[...] * cs_ref[...]
    r = g + pltpu.roll(g, QK_ROPE, axis=1)
    lane = lax.broadcasted_iota(jnp.int32, r.shape, 1)
    o_ref[...] = jnp.where(lane < QK_ROPE, r, 0.0).astype(o_ref.dtype)


def rope_shared_key(proj, cs, col_blk, *, tm=512):
    rows = proj.shape[0]
    tm = min(tm, rows)
    return pl.pallas_call(
        _kpe_kernel,
        grid=(rows // tm,),
        in_specs=[pl.BlockSpec((tm, LANES), lambda i: (i, col_blk)), pl.BlockSpec((tm, LANES), lambda i: (i, 0))],
        out_specs=pl.BlockSpec((tm, LANES), lambda i: (i, 0)),
        out_shape=jax.ShapeDtypeStruct((rows, LANES), BF16),
        compiler_params=_cparams(1),
        name="rope_shared_key",
    )(proj, cs)


def _attn_kernel(q_ref, kv_ref, kpe_ref, o_ref, *, tq, tk, hp):
    qi = pl.program_id(2)
    n_full = (qi * tq) // tk
    n_diag = max(1, tq // tk)
    row = lax.broadcasted_iota(jnp.int32, (tq, tk), 0) + qi * tq
    col0 = lax.broadcasted_iota(jnp.int32, (tq, tk), 1)
    qs = [q_ref[:, a * Q_HEAD_PAD:(a + 1) * Q_HEAD_PAD] for a in range(hp)]

    def step(j, carry, masked):
        r0 = pl.multiple_of(j * tk, tk)
        kpe = kpe_ref[pl.ds(r0, tk), :]
        out = []
        for a in range(hp):
            m, l, acc = carry[a]
            c0 = a * (QK_NOPE + V_HEAD)
            k = jnp.concatenate([kv_ref[pl.ds(r0, tk), c0:c0 + QK_NOPE], kpe], axis=1)
            v = kv_ref[pl.ds(r0, tk), c0 + QK_NOPE:c0 + QK_NOPE + V_HEAD]
            s = lax.dot_general(qs[a], k, (((1,), (1,)), ((), ())), preferred_element_type=F32)
            if masked:
                s = jnp.where(col0 + j * tk <= row, s, NEG_BIG)
            m_new = jnp.maximum(m, jnp.max(s, axis=1, keepdims=True))
            p = jnp.exp2(s - m_new)
            alpha = jnp.exp2(m - m_new)
            l = alpha * l + jnp.sum(p, axis=1, keepdims=True)
            acc = alpha * acc + jnp.dot(p.astype(BF16), v, preferred_element_type=F32)
            out.append((m_new, l, acc))
        return tuple(out)

    init = tuple((jnp.full((tq, 1), NEG_BIG, F32), jnp.zeros((tq, 1), F32), jnp.zeros((tq, V_HEAD), F32))
                 for _ in range(hp))
    carry = lax.fori_loop(0, n_full, lambda j, c: step(j, c, False), init)
    for jd in range(n_diag):
        carry = step(n_full + jd, carry, True)
    for a in range(hp):
        _, l, acc = carry[a]
        o_ref[:, a * V_HEAD:(a + 1) * V_HEAD] = (acc / l).astype(o_ref.dtype)


def mla_attention_core(q, kv, kpe, *, batch, seq, heads, tq=512, tk=512, hp=2):
    tq = min(tq, seq)
    tk = min(tk, seq)
    hp = min(hp, heads)
    assert (tk % tq == 0 or tq % tk == 0) and seq % tk == 0
    nq = seq // tq
    kern = functools.partial(_attn_kernel, tq=tq, tk=tk, hp=hp)
    return pl.pallas_call(
        kern,
        grid=(batch, heads // hp, nq),
        in_specs=[
            pl.BlockSpec((tq, hp * Q_HEAD_PAD), lambda b, h, i: (b * nq + i, h)),
            pl.BlockSpec((seq, hp * (QK_NOPE + V_HEAD)), lambda b, h, i: (b, h)),
            pl.BlockSpec((seq, LANES), lambda b, h, i: (b, 0)),
        ],
        out_specs=pl.BlockSpec((tq, hp * V_HEAD), lambda b, h, i: (b * nq + i, h)),
        out_shape=jax.ShapeDtypeStruct((batch * seq, heads * V_HEAD), BF16),
        compiler_params=_cparams(3),
        name="mla_flash_attention",
    )(q, kv, kpe)


def _gdn_gate_kernel(x_ref, w_ref, alog_ref, dtb_ref, o_ref, *, heads):
    ba = jnp.dot(x_ref[...], w_ref[...].astype(BF16), preferred_element_type=F32)
    lane = lax.broadcasted_iota(jnp.int32, ba.shape, 1)
    beta = jax.nn.sigmoid(ba)
    z = ba + dtb_ref[...]
    softplus = jnp.maximum(z, 0.0) + jnp.log1p(jnp.exp(-jnp.abs(z)))
    g = -jnp.exp(alog_ref[...]) * softplus
    o_ref[...] = jnp.where(lane < heads, beta, g)


def gdn_gates(xn, w_ba, a_log, dt_bias, *, heads, tm=512):
    rows, k = xn.shape
    tm = min(tm, rows)
    pad = LANES - 2 * heads
    w = jnp.pad(w_ba, ((0, 0), (0, pad)))
    alog = jnp.pad(a_log.astype(F32), (heads, pad)).reshape(1, LANES)
    dtb = jnp.pad(dt_bias.astype(F32), (heads, pad)).reshape(1, LANES)
    return pl.pallas_call(
        functools.partial(_gdn_gate_kernel, heads=heads),
        grid=(rows // tm,),
        in_specs=[pl.BlockSpec((tm, k), lambda i: (i, 0)), pl.BlockSpec((k, LANES), lambda i: (0, 0)),
                  pl.BlockSpec((1, LANES), lambda i: (0, 0)), pl.BlockSpec((1, LANES), lambda i: (0, 0))],
        out_specs=pl.BlockSpec((tm, LANES), lambda i: (i, 0)),
        out_shape=jax.ShapeDtypeStruct((rows, LANES), F32),
        compiler_params=_cparams(1),
        name="gdn_gates",
    )(xn, w, alog, dtb)


def _bmm(a, b):
    return lax.dot_general(a, b, (((2,), (1,)), ((0,), (0,))), preferred_element_type=F32)


def _bmm_nt(a, b):
    return lax.dot_general(a, b, (((2,), (2,)), ((0,), (0,))), preferred_element_type=F32)


def _bmm_tn(a, b):
    return lax.dot_general(a, b, (((1,), (1,)), ((0,), (0,))), preferred_element_type=F32)


def _gdn_kernel(q_ref, k_ref, v_ref, z_ref, cwq_ref, cwk_ref, cwv_ref, gate_ref, nw_ref, o_ref,
                halo, xx, lhs_s, bt_s, au_s, egl_s, st, *, hg, ts, heads):
    nc = ts // CHUNK
    cb = min(16, nc)
    hgi = pl.program_id(1)
    first = pl.program_id(2) == 0

    lane_t = lax.broadcasted_iota(jnp.int32, (ts, LANES), 1)
    ii = lax.broadcasted_iota(jnp.int32, (CHUNK, CHUNK), 0)
    jj = lax.broadcasted_iota(jnp.int32, (CHUNK, CHUNK), 1)
    incl = ii >= jj
    strict = ii > jj
    lower_ones = jnp.where(incl, 1.0, 0.0).astype(F32)
    eye = jnp.where(ii == jj, 1.0, 0.0).astype(F32)
    gates = gate_ref[...]

    def conv_silu(idx, x_ref, w_ref, c0):
        x = x_ref[:, c0:c0 + LANES]
        w = w_ref[:, c0:c0 + LANES]
        xb = xx.at[idx * hg + c0 // LANES]
        xb[0:8, :] = jnp.where(first, 0.0, halo[idx, :, c0:c0 + LANES])
        xb[8:ts + 8, :] = x
        halo[idx, :, c0:c0 + LANES] = x[ts - 8:ts, :]
        y = x * w[3:4, :]
        for s in (1, 2, 3):
            y = y + xb[8 - s:8 - s + ts, :] * w[3 - s:4 - s, :]
        return _silu(y)

    @pl.when(first)
    def _():
        st[...] = jnp.zeros(st.shape, F32)

    for hh in range(hg):
        c0 = hh * LANES
        head = hgi * hg + hh
        q = conv_silu(0, q_ref, cwq_ref, c0)
        q = q * lax.rsqrt(jnp.sum(q * q, axis=-1, keepdims=True) + RMS_EPS) * (GDN_DK ** -0.5)
        k = conv_silu(1, k_ref, cwk_ref, c0)
        k = k * lax.rsqrt(jnp.sum(k * k, axis=-1, keepdims=True) + RMS_EPS)
        v = conv_silu(2, v_ref, cwv_ref, c0)

        b_col = jnp.sum(jnp.where(lane_t == head, gates, 0.0), axis=1, keepdims=True)
        g_col = jnp.sum(jnp.where(lane_t == heads + head, gates, 0.0), axis=1, keepdims=True)
        for sb in range(nc // cb):
            r0, r1 = sb * cb * CHUNK, (sb + 1) * cb * CHUNK
            cs0, cs1 = sb * cb, (sb + 1) * cb
            b3 = b_col[r0:r1].reshape(cb, CHUNK, 1)
            g3 = g_col[r0:r1].reshape(cb, CHUNK, 1)
            g_row3 = jnp.sum(g3 * eye, axis=1, keepdims=True)
            gc3 = jnp.sum(lower_ones * g_row3, axis=2, keepdims=True)
            gc_row3 = jnp.sum(gc3 * eye, axis=1, keepdims=True)
            gl3 = gc3[:, CHUNK - 1:CHUNK, :]
            decay = jnp.where(incl, jnp.exp(jnp.where(incl, gc3 - gc_row3, 0.0)), 0.0)
            egc3 = jnp.exp(gc3)

            q3 = q[r0:r1].reshape(cb, CHUNK, LANES)
            k3 = k[r0:r1].reshape(cb, CHUNK, LANES)
            v3 = v[r0:r1].reshape(cb, CHUNK, LANES)
            q16 = q3.astype(BF16)
            k16 = k3.astype(BF16)
            lower = jnp.where(strict, _bmm_nt(k16, k16) * decay * b3, 0.0)
            p = -lower
            t_inv = eye + p
            for _ in range(5):
                p16 = p.astype(BF16)
                p = _bmm(p16, p16)
                t_inv = t_inv + _bmm(t_inv.astype(BF16), p.astype(BF16))
            vk16 = jnp.concatenate([(v3 * b3).astype(BF16), (k3 * (b3 * egc3)).astype(BF16)], axis=-1)
            uw16 = _bmm(t_inv.astype(BF16), vk16).astype(BF16)
            a16 = jnp.where(incl, _bmm_nt(q16, k16) * decay, 0.0).astype(BF16)
            awu = _bmm(a16, uw16)
            kd16 = (k3 * jnp.exp(gl3 - gc3)).astype(BF16)
            mb = _bmm_tn(kd16, uw16)
            lhs_s[hh, cs0:cs1, 0:GDN_DK, :] = mb[:, :, LANES:].astype(BF16)
            lhs_s[hh, cs0:cs1, GDN_DK:GDN_DK + CHUNK, :] = (q3 * egc3 - awu[:, :, LANES:]).astype(BF16)
            bt_s[hh, cs0:cs1] = mb[:, :, 0:LANES]
            au_s[hh, cs0:cs1] = awu[:, :, 0:LANES]
            egl_s[hh, cs0:cs1] = jnp.broadcast_to(jnp.exp(gl3), (cb, 1, LANES))

    def chunk_body(c, carry):
        s_old = [st[hh] for hh in range(hg)]
        res = [jnp.dot(lhs_s[hh, c], s_old[hh].astype(BF16), preferred_element_type=F32) for hh in range(hg)]
        for hh in range(hg):
            st[hh] = egl_s[hh, c] * s_old[hh] - res[hh][0:GDN_DK, :] + bt_s[hh, c]
            au_s[hh, c] = res[hh][GDN_DK:GDN_DK + CHUNK, :] + au_s[hh, c]
        return carry

    lax.fori_loop(0, nc, chunk_body, 0)

    nw = nw_ref[...]
    for hh in range(hg):
        c0 = hh * LANES
        o = au_s[hh].reshape(ts, LANES)
        on = o * lax.rsqrt(jnp.mean(o * o, axis=-1, keepdims=True) + RMS_EPS) * nw
        o_ref[:, c0:c0 + LANES] = (on * _silu(z_ref[:, c0:c0 + LANES])).astype(o_ref.dtype)


def gdn_delta(proj, conv_w, gates, norm_w, *, batch, seq, heads, hg=4, ts=1024):
    hg = min(hg, heads)
    ts = min(ts, seq)
    nhb = heads // hg
    nst = seq // ts
    wblk = hg * LANES
    nc = ts // CHUNK
    kern = functools.partial(_gdn_kernel, hg=hg, ts=ts, heads=heads)

    def pmap(off):
        return lambda b, h, i: (b * nst + i, off * nhb + h)

    def cmap(off):
        return lambda b, h, i: (0, off * nhb + h)

    return pl.pallas_call(
        kern,
        grid=(batch, nhb, nst),
        in_specs=[pl.BlockSpec((ts, wblk), pmap(0)), pl.BlockSpec((ts, wblk), pmap(1)),
                  pl.BlockSpec((ts, wblk), pmap(2)), pl.BlockSpec((ts, wblk), pmap(3)),
                  pl.BlockSpec((4, wblk), cmap(0)), pl.BlockSpec((4, wblk), cmap(1)), pl.BlockSpec((4, wblk), cmap(2)),
                  pl.BlockSpec((ts, LANES), lambda b, h, i: (b * nst + i, 0)),
                  pl.BlockSpec((1, LANES), lambda b, h, i: (0, 0))],
        out_specs=pl.BlockSpec((ts, wblk), lambda b, h, i: (b * nst + i, h)),
        out_shape=jax.ShapeDtypeStruct((batch * seq, heads * GDN_DV), BF16),
        scratch_shapes=[pltpu.VMEM((3, 8, wblk), F32),
                        pltpu.VMEM((3 * hg, ts + 8, LANES), F32),
                        pltpu.VMEM((hg, nc, GDN_DK + CHUNK, LANES), BF16),
                        pltpu.VMEM((hg, nc, GDN_DK, GDN_DV), F32),
                        pltpu.VMEM((hg, nc, CHUNK, GDN_DV), F32),
                        pltpu.VMEM((hg, nc, 1, LANES), F32),
                        pltpu.VMEM((hg, GDN_DK, GDN_DV), F32)],
        compiler_params=_cparams(3),
        name="gdn_delta_rule",
    )(proj, proj, proj, proj, conv_w, conv_w, conv_w, gates, norm_w.reshape(1, LANES).astype(F32))


def _router_kernel(x_ref, g_ref, w_ref, b_ref, o_ref):
    xf = x_ref[...]
    var = jnp.mean(xf * xf, axis=-1, keepdims=True)
    xn = xf * lax.rsqrt(var + RMS_EPS) * g_ref[...]
    w = w_ref[...]
    x_hi = xn.astype(BF16)
    x_lo = (xn - x_hi.astype(F32)).astype(BF16)
    w_hi = w.astype(BF16)
    w_lo = (w - w_hi.astype(F32)).astype(BF16)
    acc = jnp.dot(x_hi, w_hi, preferred_element_type=F32)
    acc = acc + jnp.dot(x_lo, w_hi, preferred_element_type=F32) + jnp.dot(x_hi, w_lo, preferred_element_type=F32)
    o_ref[...] = acc + b_ref[...]


def router_logits(h, gain, router_w, router_b, *, tm=512):
    rows, d = h.shape
    tm = min(tm, rows)
    n_e = router_w.shape[1]
    w = jnp.pad(router_w, ((0, 0), (0, LANES - n_e)))
    b = jnp.pad(router_b.astype(F32), (0, LANES - n_e)).reshape(1, LANES)
    return pl.pallas_call(
        _router_kernel,
        grid=(rows // tm,),
        in_specs=[pl.BlockSpec((tm, d), lambda i: (i, 0)), pl.BlockSpec((1, d), lambda i: (0, 0)),
                  pl.BlockSpec((d, LANES), lambda i: (0, 0)), pl.BlockSpec((1, LANES), lambda i: (0, 0))],
        out_specs=pl.BlockSpec((tm, LANES), lambda i: (i, 0)),
        out_shape=jax.ShapeDtypeStruct((rows, LANES), F32),
        compiler_params=_cparams(1),
        name="moe_router",
    )(h, gain.reshape(1, d).astype(F32), w, b)


def _row_copy(src_hbm, dst, sem, src_row, dst_row):
    return pltpu.make_async_copy(src_hbm.at[pl.ds(src_row, 1)], dst.at[pl.ds(dst_row, 1)], sem)


def _gather_norm_kernel(tok_ref, h_hbm, g_ref, o_ref, buf, sem, *, tm, nblk):
    rb = pl.program_id(0)

    def issue(blk, slot):
        def one(r, carry):
            _row_copy(h_hbm, buf.at[slot], sem.at[slot], tok_ref[blk * tm + r], r).start()
            return carry
        lax.fori_loop(0, tm, one, 0, unroll=8)

    @pl.when(rb == 0)
    def _():
        issue(0, 0)

    @pl.when(rb + 1 < nblk)
    def _():
        issue(rb + 1, (rb + 1) % 2)

    slot = rb % 2
    pltpu.make_async_copy(h_hbm.at[pl.ds(0, tm)], buf.at[slot], sem.at[slot]).wait()
    xf = buf[slot]
    var = jnp.mean(xf * xf, axis=-1, keepdims=True)
    o_ref[...] = (xf * lax.rsqrt(var + RMS_EPS) * g_ref[...]).astype(o_ref.dtype)


def gather_norm(h, gain, slot_tok, *, tm):
    rows, d = h.shape
    n_slots = slot_tok.shape[0]
    nblk = n_slots // tm
    kern = functools.partial(_gather_norm_kernel, tm=tm, nblk=nblk)
    return pl.pallas_call(
        kern,
        grid_spec=pltpu.PrefetchScalarGridSpec(
            num_scalar_prefetch=1,
            grid=(nblk,),
            in_specs=[pl.BlockSpec(memory_space=pl.ANY), pl.BlockSpec((1, d), lambda i, t: (0, 0))],
            out_specs=pl.BlockSpec((tm, d), lambda i, t: (i, 0)),
            scratch_shapes=[pltpu.VMEM((2, tm, d), F32), pltpu.SemaphoreType.DMA((2,))],
        ),
        out_shape=jax.ShapeDtypeStruct((n_slots, d), BF16),
        compiler_params=_cparams(1),
        name="moe_gather_norm",
    )(slot_tok, h, gain.reshape(1, d).astype(F32))


def _combine_kernel(slot_ref, y_hbm, h_ref, gate_ref, g_ref, o_ref, buf, sem, *, tc, nblk):
    i = pl.program_id(0)

    def issue(blk, slot):
        def one(r, carry):
            for kk in range(TOP_K):
                _row_copy(y_hbm, buf.at[slot, kk], sem.at[slot], slot_ref[(blk * tc + r) * TOP_K + kk], r).start()
            return carry
        lax.fori_loop(0, tc, one, 0, unroll=8)

    @pl.when(i == 0)
    def _():
        issue(0, 0)

    @pl.when(i + 1 < nblk)
    def _():
        issue(i + 1, (i + 1) % 2)

    slot = i % 2
    for kk in range(TOP_K):
        pltpu.make_async_copy(y_hbm.at[pl.ds(0, tc)], buf.at[slot, kk], sem.at[slot]).wait()
    gate = gate_ref[...]
    xf = h_ref[...] + gate[:, 0:1] * buf[slot, 0] + gate[:, 1:2] * buf[slot, 1]
    var = jnp.mean(xf * xf, axis=-1, keepdims=True)
    o_ref[...] = xf * lax.rsqrt(var + RMS_EPS) * g_ref[...]


def combine_norm(h, y, tok_slots, gates, final_gain, *, tc=256):
    rows, d = h.shape
    tc = min(tc, rows)
    nblk = rows // tc
    kern = functools.partial(_combine_kernel, tc=tc, nblk=nblk)
    return pl.pallas_call(
        kern,
        grid_spec=pltpu.PrefetchScalarGridSpec(
            num_scalar_prefetch=1,
            grid=(nblk,),
            in_specs=[pl.BlockSpec(memory_space=pl.ANY), pl.BlockSpec((tc, d), lambda i, s: (i, 0)),
                      pl.BlockSpec((tc, TOP_K), lambda i, s: (i, 0)), pl.BlockSpec((1, d), lambda i, s: (0, 0))],
            out_specs=pl.BlockSpec((tc, d), lambda i, s: (i, 0)),
            scratch_shapes=[pltpu.VMEM((2, TOP_K, tc, d), F32), pltpu.SemaphoreType.DMA((2,))],
        ),
        out_shape=jax.ShapeDtypeStruct((rows, d), F32),
        compiler_params=_cparams(1),
        name="moe_combine_norm",
    )(tok_slots, y, h, gates, final_gain.reshape(1, d).astype(F32))


def _route(logits, n_experts, tm):
    n_tok = logits.shape[0]
    n_asg = n_tok * TOP_K
    nblk = n_asg // tm + n_experts
    top_logit, top_idx = lax.top_k(logits[:, :n_experts], TOP_K)
    gates = jax.nn.softmax(top_logit, axis=-1)
    flat_e = top_idx.reshape(n_asg).astype(jnp.int32)
    onehot = (flat_e[:, None] == jnp.arange(n_experts, dtype=jnp.int32)[None, :]).astype(jnp.int32)
    csum = jnp.cumsum(onehot, axis=0)
    rank = jnp.sum(csum * onehot, axis=1) - 1
    counts = csum[-1]
    padded = (counts + tm - 1) // tm * tm
    pad_end = jnp.cumsum(padded)
    pad_start = pad_end - padded
    dest = (pad_start[flat_e] + rank).astype(jnp.int32)
    slot_tok = jnp.zeros((nblk * tm,), jnp.int32).at[dest].set(jnp.arange(n_asg, dtype=jnp.int32) // TOP_K)
    nused = (pad_end[-1] // tm).astype(jnp.int32)
    blk = jnp.arange(nblk, dtype=jnp.int32)
    be = jnp.minimum(jnp.searchsorted(pad_end, blk * tm, side="right"), n_experts - 1).astype(jnp.int32)
    be = jnp.where(blk < nused, be, be[jnp.maximum(nused - 1, 0)])
    seg_end = pad_end[be] // tm
    nxt = jnp.where(seg_end < nused, be[jnp.minimum(seg_end, nblk - 1)], -1).astype(jnp.int32)
    return slot_tok, gates, dest, be, nused.reshape(1), nxt


def _rope_table(positions):
    inv_freq = ROPE_THETA ** (-jnp.arange(0, QK_ROPE, 2, dtype=F32) / QK_ROPE)
    ang = positions.astype(F32)[..., None] * inv_freq
    cos, sin = jnp.cos(ang), jnp.sin(ang)
    return jnp.concatenate([cos, cos, sin, sin], axis=-1).reshape(-1, 2 * QK_ROPE)


def _rot_cols(w):
    half = w.shape[-1] // 2
    return jnp.concatenate([-w[..., half:], w[..., :half]], axis=-1)


def _mla_layer(h, cs, ln, w_in, q_norm, w_qb, kv_norm, w_kvb, w_o, *, batch, seq):
    d = h.shape[1]
    q_lora = q_norm.shape[0]
    kv_lora = kv_norm.shape[0]
    heads = w_o.shape[0] // V_HEAD
    assert q_lora == kv_lora and q_lora % LANES == 0
    w_rope = w_in[:, q_lora + kv_lora:]
    w_in_p = jnp.concatenate([w_in, _rot_cols(w_rope)], axis=1)
    n_in = w_in_p.shape[1]
    wq = w_qb.reshape(q_lora, heads, QK_NOPE + QK_ROPE)
    wq_p = jnp.concatenate([wq, _rot_cols(wq[..., QK_NOPE:])], axis=-1).reshape(q_lora, heads * Q_HEAD_PAD)

    proj = gmm(h, [w_in_p[None]], n_out=n_in, tm=512, tn=n_in, out_dtype=F32, gain=ln, name="mla_in_proj")
    scale = LOG2_E * (QK_NOPE + QK_ROPE) ** -0.5
    q = gmm(proj, [wq_p[None]], n_out=heads * Q_HEAD_PAD, tm=2048, tn=1024, out_dtype=BF16, mode="rope", cs=cs,
            scale=scale, gain=q_norm, x_col_blk=0, name="mla_q_proj")
    kv = gmm(proj, [w_kvb[None]], n_out=heads * (QK_NOPE + V_HEAD), tm=2048, tn=1024, out_dtype=BF16,
             gain=kv_norm, x_col_blk=1, name="mla_kv_proj")
    kpe = rope_shared_key(proj, cs, (q_lora + kv_lora) // LANES)
    o = mla_attention_core(q, kv, kpe, batch=batch, seq=seq, heads=heads)
    return gmm(o, [w_o[None]], n_out=d, tm=1024, tn=1024, out_dtype=F32, residual=h, name="mla_out_proj")


def _ffn_layer(h, ln, w_gate, w_up, w_down):
    d = h.shape[1]
    f = w_gate.shape[1]
    hn = rmsnorm(h, ln)
    mid = gmm(hn, [w_gate[None], w_up[None]], n_out=f, tm=1024, tn=1024 if f % 1024 == 0 else f // 7,
              out_dtype=BF16, mode="swiglu", name="ffn_up")
    return gmm(mid, [w_down[None]], n_out=d, tm=512, tn=512, out_dtype=F32, residual=h, name="ffn_down")


def _gdn_layer(h, ln, w_in, conv_w, a_log, dt_bias, norm_w, w_o, *, batch, seq):
    d = h.shape[1]
    heads = a_log.shape[0]
    n_main = 4 * heads * LANES
    hn = rmsnorm(h, ln)
    proj = gmm(hn, [w_in[None]], n_out=n_main, tm=1024, tn=1024, out_dtype=F32, name="gdn_in_proj")
    gates = gdn_gates(hn, w_in[:, n_main:], a_log, dt_bias, heads=heads)
    o = gdn_delta(proj, conv_w, gates, norm_w, batch=batch, seq=seq, heads=heads)
    return gmm(o, [w_o[None]], n_out=d, tm=1024, tn=1024, out_dtype=F32, residual=h, name="gdn_out_proj")


def _moe_layer_final(h, ln, router_w, router_b, w_gate, w_up, w_down, final_gain, *, tm=512):
    rows, d = h.shape
    n_e, _, f = w_gate.shape
    tm = min(tm, rows // 4)
    logits = router_logits(h, ln, router_w, router_b)
    slot_tok, gates, dest, be, nused, nxt = _route(logits, n_e, tm)
    xs = gather_norm(h, ln, slot_tok, tm=tm)
    mid = gmm(xs, [w_gate, w_up], n_out=f, tm=tm, tn=1024 if f % 1024 == 0 else f // 7, out_dtype=BF16,
              be=be, nused=nused, nxt=nxt, mode="swiglu", name="moe_up")
    y = gmm(mid, [w_down], n_out=d, tm=tm, tn=512, out_dtype=F32, be=be, nused=nused, nxt=nxt, name="moe_down")
    return combine_norm(h, y, dest, gates, final_gain)


def kernel(x, positions, ln_mix_mla, mla_w_in, mla_q_norm, mla_w_qb, mla_kv_norm, mla_w_kvb, mla_w_o, ln_ffn_dense, ffn_w_gate, ffn_w_up, ffn_w_down, ln_mix_gdn, gdn_w_in, gdn_conv_w, gdn_a_log, gdn_dt_bias, gdn_norm, gdn_w_o, ln_ffn_moe, moe_router, moe_router_bias, moe_w_gate, moe_w_up, moe_w_down, final_norm):
    batch, seq, d = x.shape
    assert ln_mix_mla.shape[0] == 1 and ln_mix_gdn.shape[0] == 1, "two-layer trunk: one MLA and one DeltaNet layer"
    h = x.reshape(batch * seq, d)
    cs = _rope_table(positions)
    h = _mla_layer(h, cs, ln_mix_mla[0], mla_w_in[0], mla_q_norm[0], mla_w_qb[0], mla_kv_norm[0], mla_w_kvb[0],
                   mla_w_o[0], batch=batch, seq=seq)
    h = _ffn_layer(h, ln_ffn_dense[0], ffn_w_gate[0], ffn_w_up[0], ffn_w_down[0])
    h = _gdn_layer(h, ln_mix_gdn[0], gdn_w_in[0], gdn_conv_w[0], gdn_a_log[0], gdn_dt_bias[0], gdn_norm[0],
                   gdn_w_o[0], batch=batch, seq=seq)
    out = _moe_layer_final(h, ln_ffn_moe[0], moe_router[0], moe_router_bias[0], moe_w_gate[0], moe_w_up[0],
                           moe_w_down[0], final_norm)
    return out.reshape(batch, seq, d)
```

```python
import functools

import jax
import jax.numpy as jnp
from jax import lax
from jax.experimental import pallas as pl
from jax.experimental.pallas import tpu as pltpu

F32 = jnp.float32
BF16 = jnp.bfloat16

RMS_EPS = 1e-6
ROPE_THETA = 10000.0
QK_NOPE = 128
QK_ROPE = 64
V_HEAD = 128
Q_HEAD_PAD = 256
GDN_DK = 128
GDN_DV = 128
CHUNK = 64
TOP_K = 2
LANES = 128
VMEM_LIMIT = 56 * 1024 * 1024
NEG_BIG = -1e30
ROWS_PER_ISSUE = 8
LOG2_E = 1.4426950408889634


def _cparams(n_axes):
    return pltpu.CompilerParams(dimension_semantics=("arbitrary",) * n_axes, vmem_limit_bytes=VMEM_LIMIT)


def _silu(x):
    return x * jax.nn.sigmoid(x)


def _rmsnorm_kernel(x_ref, g_ref, o_ref):
    xf = x_ref[...].astype(F32)
    var = jnp.mean(xf * xf, axis=-1, keepdims=True)
    o_ref[...] = (xf * lax.rsqrt(var + RMS_EPS) * g_ref[...]).astype(o_ref.dtype)


def rmsnorm(x, gain, *, tm=512, out_dtype=BF16):
    rows = x.shape[0]
    k = gain.shape[-1]
    tm = min(tm, rows)
    return pl.pallas_call(
        _rmsnorm_kernel,
        grid=(rows // tm,),
        in_specs=[pl.BlockSpec((tm, k), lambda i: (i, 0)), pl.BlockSpec((1, k), lambda i: (0, 0))],
        out_specs=pl.BlockSpec((tm, k), lambda i: (i, 0)),
        out_shape=jax.ShapeDtypeStruct((rows, k), out_dtype),
        compiler_params=_cparams(1),
        name="rmsnorm",
    )(x, gain.reshape(1, k).astype(F32))


def _gmm_kernel(be_ref, nu_ref, nxt_ref, x_ref, *rest, n_w, n_tiles, tn, mode, has_gain, has_res, scale):
    gain_ref = res_ref = cs_ref = None
    if has_gain:
        gain_ref, rest = rest[0], rest[1:]
    w_hbm = rest[:n_w]
    rest = rest[n_w:]
    if has_res:
        res_ref, rest = rest[0], rest[1:]
    if mode == "rope":
        cs_ref, rest = rest[0], rest[1:]
    o_ref, stage, wbf, sem = rest
    n = pl.program_id(0)
    rb = pl.program_id(1)

    def weight_copy(i, expert, n_tile):
        col = pl.multiple_of(n_tile * tn, tn)
        return pltpu.make_async_copy(w_hbm[i].at[expert, :, pl.ds(col, tn)], stage.at[i], sem.at[i])

    def start_weights(expert, n_tile):
        for i in range(n_w):
            weight_copy(i, expert, n_tile).start()

    prev = be_ref[jnp.maximum(rb - 1, 0)]
    changed = jnp.logical_or(rb == 0, be_ref[rb] != prev)

    @pl.when(changed)
    def _():
        @pl.when(jnp.logical_and(n == 0, rb == 0))
        def _():
            start_weights(be_ref[0], 0)

        for i in range(n_w):
            weight_copy(i, 0, 0).wait()
            wbf[i] = stage[i].astype(BF16)
        nxt = nxt_ref[rb]

        @pl.when(nxt >= 0)
        def _():
            start_weights(nxt, n)

        @pl.when(jnp.logical_and(nxt < 0, n + 1 < n_tiles))
        def _():
            start_weights(be_ref[0], n + 1)

    @pl.when(rb < nu_ref[0])
    def _():
        x = x_ref[...]
        if has_gain:
            var = jnp.mean(x * x, axis=-1, keepdims=True)
            x = (x * lax.rsqrt(var + RMS_EPS) * gain_ref[...]).astype(BF16)
        if mode == "swiglu":
            a = jnp.dot(x, wbf[0], preferred_element_type=F32)
            b = jnp.dot(x, wbf[1], preferred_element_type=F32)
            o_ref[...] = (_silu(a) * b).astype(o_ref.dtype)
        else:
            acc = jnp.dot(x, wbf[0], preferred_element_type=F32)
            if has_res:
                acc = acc + res_ref[...]
            if mode == "rope":
                cs = cs_ref[...]
                for hh in range(acc.shape[1] // Q_HEAD_PAD):
                    c0 = hh * Q_HEAD_PAD
                    o_ref[:, c0:c0 + QK_NOPE] = (acc[:, c0:c0 + QK_NOPE] * scale).astype(o_ref.dtype)
                    g = acc[:, c0 + QK_NOPE:c0 + Q_HEAD_PAD] * cs
                    r = g + pltpu.roll(g, QK_ROPE, axis=1)
                    o_ref[:, c0 + QK_NOPE:c0 + Q_HEAD_PAD] = (r * scale).astype(o_ref.dtype)
            else:
                o_ref[...] = acc.astype(o_ref.dtype)

    @pl.when(rb >= nu_ref[0])
    def _():
        o_ref[...] = jnp.zeros(o_ref.shape, o_ref.dtype)


def gmm(x, ws, *, n_out, tm, tn, out_dtype, be=None, nused=None, nxt=None, mode="plain", gain=None, x_col_blk=0,
        residual=None, cs=None, scale=None, name="gmm"):
    rows = x.shape[0]
    k = ws[0].shape[1]
    tm = min(tm, rows)
    tn = min(tn, n_out)
    nblk = rows // tm
    n_tiles = n_out // tn
    if be is None:
        be = jnp.zeros((nblk,), jnp.int32)
        nused = jnp.full((1,), nblk, jnp.int32)
        nxt = jnp.full((nblk,), -1, jnp.int32)
    n_w = len(ws)

    def x_map(n, rb, be_r, nu_r, nx_r):
        return (jnp.minimum(rb, nu_r[0] - 1), x_col_blk)

    def row_map(n, rb, be_r, nu_r, nx_r):
        return (rb, 0)

    def out_map(n, rb, be_r, nu_r, nx_r):
        return (rb, n)

    in_specs = [pl.BlockSpec((tm, k), x_map)]
    args = [x]
    if gain is not None:
        in_specs.append(pl.BlockSpec((1, k), lambda n, rb, be_r, nu_r, nx_r: (0, 0)))
        args.append(gain.reshape(1, k).astype(F32))
    in_specs += [pl.BlockSpec(memory_space=pl.ANY) for _ in ws]
    args += list(ws)
    if residual is not None:
        in_specs.append(pl.BlockSpec((tm, tn), out_map))
        args.append(residual)
    if mode == "rope":
        in_specs.append(pl.BlockSpec((tm, LANES), row_map))
        args.append(cs)
    kern = functools.partial(_gmm_kernel, n_w=n_w, n_tiles=n_tiles, tn=tn, mode=mode, has_gain=gain is not None,
                             has_res=residual is not None, scale=scale)
    return pl.pallas_call(
        kern,
        grid_spec=pltpu.PrefetchScalarGridSpec(
            num_scalar_prefetch=3,
            grid=(n_tiles, nblk),
            in_specs=in_specs,
            out_specs=pl.BlockSpec((tm, tn), out_map),
            scratch_shapes=[pltpu.VMEM((n_w, k, tn), F32), pltpu.VMEM((n_w, k, tn), BF16),
                            pltpu.SemaphoreType.DMA((n_w,))],
        ),
        out_shape=jax.ShapeDtypeStruct((rows, n_out), out_dtype),
        compiler_params=_cparams(2),
        name=name,
    )(be, nused, nxt, *args)


def _kpe_kernel(x_ref, cs_ref, o_ref):
    g = x_ref[...] * cs_ref[...]
    r = g + pltpu.roll(g, QK_ROPE, axis=1)
    lane = lax.broadcasted_iota(jnp.int32, r.shape, 1)
    o_ref[...] = jnp.where(lane < QK_ROPE, r, 0.0).astype(o_ref.dtype)


def rope_shared_key(proj, cs, col_blk, *, tm=512):
    rows = proj.shape[0]
    tm = min(tm, rows)
    return pl.pallas_call(
        _kpe_kernel,
        grid=(rows // tm,),
        in_specs=[pl.BlockSpec((tm, LANES), lambda i: (i, col_blk)), pl.BlockSpec((tm, LANES), lambda i: (i, 0))],
        out_specs=pl.BlockSpec((tm, LANES), lambda i: (i, 0)),
        out_shape=jax.ShapeDtypeStruct((rows, LANES), BF16),
        compiler_params=_cparams(1),
        name="rope_shared_key",
    )(proj, cs)


def _attn_kernel(q_ref, kv_ref, kpe_ref, o_ref, *, tq, tk, hp):
    qi = pl.program_id(2)
    n_full = (qi * tq) // tk
    n_diag = max(1, tq // tk)
    row = lax.broadcasted_iota(jnp.int32, (tq, tk), 0) + qi * tq
    col0 = lax.broadcasted_iota(jnp.int32, (tq, tk), 1)
    qs = [q_ref[:, a * Q_HEAD_PAD:(a + 1) * Q_HEAD_PAD] for a in range(hp)]

    def step(j, carry, masked):
        r0 = pl.multiple_of(j * tk, tk)
        kpe = kpe_ref[pl.ds(r0, tk), :]
        out = []
        for a in range(hp):
            m, l, acc = carry[a]
            c0 = a * (QK_NOPE + V_HEAD)
            k = jnp.concatenate([kv_ref[pl.ds(r0, tk), c0:c0 + QK_NOPE], kpe], axis=1)
            v = kv_ref[pl.ds(r0, tk), c0 + QK_NOPE:c0 + QK_NOPE + V_HEAD]
            s = lax.dot_general(qs[a], k, (((1,), (1,)), ((), ())), preferred_element_type=F32)
            if masked:
                s = jnp.where(col0 + j * tk <= row, s, NEG_BIG)
            m_new = jnp.maximum(m, jnp.max(s, axis=1, keepdims=True))
            p = jnp.exp2(s - m_new)
            alpha = jnp.exp2(m - m_new)
            l = alpha * l + jnp.sum(p, axis=1, keepdims=True)
            acc = alpha * acc + jnp.dot(p.astype(BF16), v, preferred_element_type=F32)
            out.append((m_new, l, acc))
        return tuple(out)

    init = tuple((jnp.full((tq, 1), NEG_BIG, F32), jnp.zeros((tq, 1), F32), jnp.zeros((tq, V_HEAD), F32))
                 for _ in range(hp))
    carry = lax.fori_loop(0, n_full, lambda j, c: step(j, c, False), init)
    for jd in range(n_diag):
        carry = step(n_full + jd, carry, True)
    for a in range(hp):
        _, l, acc = carry[a]
        o_ref[:, a * V_HEAD:(a + 1) * V_HEAD] = (acc / l).astype(o_ref.dtype)


def mla_attention_core(q, kv, kpe, *, batch, seq, heads, tq=512, tk=512, hp=4):
    tq = min(tq, seq)
    tk = min(tk, seq)
    hp = min(hp, heads)
    assert (tk % tq == 0 or tq % tk == 0) and seq % tk == 0
    nq = seq // tq
    kern = functools.partial(_attn_kernel, tq=tq, tk=tk, hp=hp)
    return pl.pallas_call(
        kern,
        grid=(batch, heads // hp, nq),
        in_specs=[
            pl.BlockSpec((tq, hp * Q_HEAD_PAD), lambda b, h, i: (b * nq + i, h)),
            pl.BlockSpec((seq, hp * (QK_NOPE + V_HEAD)), lambda b, h, i: (b, h)),
            pl.BlockSpec((seq, LANES), lambda b, h, i: (b, 0)),
        ],
        out_specs=pl.BlockSpec((tq, hp * V_HEAD), lambda b, h, i: (b * nq + i, h)),
        out_shape=jax.ShapeDtypeStruct((batch * seq, heads * V_HEAD), BF16),
        compiler_params=_cparams(3),
        name="mla_flash_attention",
    )(q, kv, kpe)


def _gdn_gate_kernel(x_ref, w_ref, alog_ref, dtb_ref, o_ref, *, heads):
    ba = jnp.dot(x_ref[...], w_ref[...].astype(BF16), preferred_element_type=F32)
    lane = lax.broadcasted_iota(jnp.int32, ba.shape, 1)
    beta = jax.nn.sigmoid(ba)
    z = ba + dtb_ref[...]
    softplus = jnp.maximum(z, 0.0) + jnp.log1p(jnp.exp(-jnp.abs(z)))
    g = -jnp.exp(alog_ref[...]) * softplus
    o_ref[...] = jnp.where(lane < heads, beta, g)


def gdn_gates(xn, w_ba, a_log, dt_bias, *, heads, tm=512):
    rows, k = xn.shape
    tm = min(tm, rows)
    pad = LANES - 2 * heads
    w = jnp.pad(w_ba, ((0, 0), (0, pad)))
    alog = jnp.pad(a_log.astype(F32), (heads, pad)).reshape(1, LANES)
    dtb = jnp.pad(dt_bias.astype(F32), (heads, pad)).reshape(1, LANES)
    return pl.pallas_call(
        functools.partial(_gdn_gate_kernel, heads=heads),
        grid=(rows // tm,),
        in_specs=[pl.BlockSpec((tm, k), lambda i: (i, 0)), pl.BlockSpec((k, LANES), lambda i: (0, 0)),
                  pl.BlockSpec((1, LANES), lambda i: (0, 0)), pl.BlockSpec((1, LANES), lambda i: (0, 0))],
        out_specs=pl.BlockSpec((tm, LANES), lambda i: (i, 0)),
        out_shape=jax.ShapeDtypeStruct((rows, LANES), F32),
        compiler_params=_cparams(1),
        name="gdn_gates",
    )(xn, w, alog, dtb)


def _bmm(a, b):
    return lax.dot_general(a, b, (((2,), (1,)), ((0,), (0,))), preferred_element_type=F32)


def _bmm_nt(a, b):
    return lax.dot_general(a, b, (((2,), (2,)), ((0,), (0,))), preferred_element_type=F32)


def _bmm_tn(a, b):
    return lax.dot_general(a, b, (((1,), (1,)), ((0,), (0,))), preferred_element_type=F32)


def _gdn_kernel(q_ref, k_ref, v_ref, z_ref, cwq_ref, cwk_ref, cwv_ref, gate_ref, nw_ref, o_ref,
                halo, xx, lhs_s, bt_s, au_s, egl_s, st, *, hg, ts, heads):
    nc = ts // CHUNK
    cb = min(16, nc)
    hgi = pl.program_id(1)
    first = pl.program_id(2) == 0

    lane_t = lax.broadcasted_iota(jnp.int32, (ts, LANES), 1)
    ii = lax.broadcasted_iota(jnp.int32, (CHUNK, CHUNK), 0)
    jj = lax.broadcasted_iota(jnp.int32, (CHUNK, CHUNK), 1)
    incl = ii >= jj
    strict = ii > jj
    lower_ones = jnp.where(incl, 1.0, 0.0).astype(F32)
    eye = jnp.where(ii == jj, 1.0, 0.0).astype(F32)
    gates = gate_ref[...]

    def conv_silu(idx, x_ref, w_ref, c0):
        x = x_ref[:, c0:c0 + LANES]
        w = w_ref[:, c0:c0 + LANES]
        xb = xx.at[idx * hg + c0 // LANES]
        xb[0:8, :] = jnp.where(first, 0.0, halo[idx, :, c0:c0 + LANES])
        xb[8:ts + 8, :] = x
        halo[idx, :, c0:c0 + LANES] = x[ts - 8:ts, :]
        y = x * w[3:4, :]
        for s in (1, 2, 3):
            y = y + xb[8 - s:8 - s + ts, :] * w[3 - s:4 - s, :]
        return _silu(y)

    @pl.when(first)
    def _():
        st[...] = jnp.zeros(st.shape, F32)

    for hh in range(hg):
        c0 = hh * LANES
        head = hgi * hg + hh
        q = conv_silu(0, q_ref, cwq_ref, c0)
        q = q * lax.rsqrt(jnp.sum(q * q, axis=-1, keepdims=True) + RMS_EPS) * (GDN_DK ** -0.5)
        k = conv_silu(1, k_ref, cwk_ref, c0)
        k = k * lax.rsqrt(jnp.sum(k * k, axis=-1, keepdims=True) + RMS_EPS)
        v = conv_silu(2, v_ref, cwv_ref, c0)

        b_col = jnp.sum(jnp.where(lane_t == head, gates, 0.0), axis=1, keepdims=True)
        g_col = jnp.sum(jnp.where(lane_t == heads + head, gates, 0.0), axis=1, keepdims=True)
        for sb in range(nc // cb):
            r0, r1 = sb * cb * CHUNK, (sb + 1) * cb * CHUNK
            cs0, cs1 = sb * cb, (sb + 1) * cb
            b3 = b_col[r0:r1].reshape(cb, CHUNK, 1)
            g3 = g_col[r0:r1].reshape(cb, CHUNK, 1)
            g_row3 = jnp.sum(g3 * eye, axis=1, keepdims=True)
            gc3 = jnp.sum(lower_ones * g_row3, axis=2, keepdims=True)
            gc_row3 = jnp.sum(gc3 * eye, axis=1, keepdims=True)
            gl3 = gc3[:, CHUNK - 1:CHUNK, :]
            decay = jnp.where(incl, jnp.exp(jnp.where(incl, gc3 - gc_row3, 0.0)), 0.0)
            egc3 = jnp.exp(gc3)

            q3 = q[r0:r1].reshape(cb, CHUNK, LANES)
            k3 = k[r0:r1].reshape(cb, CHUNK, LANES)
            v3 = v[r0:r1].reshape(cb, CHUNK, LANES)
            q16 = q3.astype(BF16)
            k16 = k3.astype(BF16)
            lower = jnp.where(strict, _bmm_nt(k16, k16) * decay * b3, 0.0)
            p = -lower
            t_inv = eye + p
            for _ in range(5):
                p16 = p.astype(BF16)
                p = _bmm(p16, p16)
                t_inv = t_inv + _bmm(t_inv.astype(BF16), p.astype(BF16))
            vk16 = jnp.concatenate([(v3 * b3).astype(BF16), (k3 * (b3 * egc3)).astype(BF16)], axis=-1)
            uw16 = _bmm(t_inv.astype(BF16), vk16).astype(BF16)
            a16 = jnp.where(incl, _bmm_nt(q16, k16) * decay, 0.0).astype(BF16)
            awu = _bmm(a16, uw16)
            kd16 = (k3 * jnp.exp(gl3 - gc3)).astype(BF16)
            mb = _bmm_tn(kd16, uw16)
            lhs_s[hh, cs0:cs1, 0:GDN_DK, :] = mb[:, :, LANES:].astype(BF16)
            lhs_s[hh, cs0:cs1, GDN_DK:GDN_DK + CHUNK, :] = (q3 * egc3 - awu[:, :, LANES:]).astype(BF16)
            bt_s[hh, cs0:cs1] = mb[:, :, 0:LANES]
            au_s[hh, cs0:cs1] = awu[:, :, 0:LANES]
            egl_s[hh, cs0:cs1] = jnp.broadcast_to(jnp.exp(gl3), (cb, 1, LANES))

    def chunk_body(c, carry):
        s_old = [st[hh] for hh in range(hg)]
        res = [jnp.dot(lhs_s[hh, c], s_old[hh].astype(BF16), preferred_element_type=F32) for hh in range(hg)]
        for hh in range(hg):
            st[hh] = egl_s[hh, c] * s_old[hh] - res[hh][0:GDN_DK, :] + bt_s[hh, c]
            au_s[hh, c] = res[hh][GDN_DK:GDN_DK + CHUNK, :] + au_s[hh, c]
        return carry

    lax.fori_loop(0, nc, chunk_body, 0)

    nw = nw_ref[...]
    for hh in range(hg):
        c0 = hh * LANES
        o = au_s[hh].reshape(ts, LANES)
        on = o * lax.rsqrt(jnp.mean(o * o, axis=-1, keepdims=True) + RMS_EPS) * nw
        o_ref[:, c0:c0 + LANES] = (on * _silu(z_ref[:, c0:c0 + LANES])).astype(o_ref.dtype)


def gdn_delta(proj, conv_w, gates, norm_w, *, batch, seq, heads, hg=4, ts=1024):
    hg = min(hg, heads)
    ts = min(ts, seq)
    nhb = heads // hg
    nst = seq // ts
    wblk = hg * LANES
    nc = ts // CHUNK
    kern = functools.partial(_gdn_kernel, hg=hg, ts=ts, heads=heads)

    def pmap(off):
        return lambda b, h, i: (b * nst + i, off * nhb + h)

    def cmap(off):
        return lambda b, h, i: (0, off * nhb + h)

    return pl.pallas_call(
        kern,
        grid=(batch, nhb, nst),
        in_specs=[pl.BlockSpec((ts, wblk), pmap(0)), pl.BlockSpec((ts, wblk), pmap(1)),
                  pl.BlockSpec((ts, wblk), pmap(2)), pl.BlockSpec((ts, wblk), pmap(3)),
                  pl.BlockSpec((4, wblk), cmap(0)), pl.BlockSpec((4, wblk), cmap(1)), pl.BlockSpec((4, wblk), cmap(2)),
                  pl.BlockSpec((ts, LANES), lambda b, h, i: (b * nst + i, 0)),
                  pl.BlockSpec((1, LANES), lambda b, h, i: (0, 0))],
        out_specs=pl.BlockSpec((ts, wblk), lambda b, h, i: (b * nst + i, h)),
        out_shape=jax.ShapeDtypeStruct((batch * seq, heads * GDN_DV), BF16),
        scratch_shapes=[pltpu.VMEM((3, 8, wblk), F32),
                        pltpu.VMEM((3 * hg, ts + 8, LANES), F32),
                        pltpu.VMEM((hg, nc, GDN_DK + CHUNK, LANES), BF16),
                        pltpu.VMEM((hg, nc, GDN_DK, GDN_DV), F32),
                        pltpu.VMEM((hg, nc, CHUNK, GDN_DV), F32),
                        pltpu.VMEM((hg, nc, 1, LANES), F32),
                        pltpu.VMEM((hg, GDN_DK, GDN_DV), F32)],
        compiler_params=_cparams(3),
        name="gdn_delta_rule",
    )(proj, proj, proj, proj, conv_w, conv_w, conv_w, gates, norm_w.reshape(1, LANES).astype(F32))


def _router_kernel(x_ref, g_ref, w_ref, b_ref, o_ref):
    xf = x_ref[...]
    var = jnp.mean(xf * xf, axis=-1, keepdims=True)
    xn = xf * lax.rsqrt(var + RMS_EPS) * g_ref[...]
    w = w_ref[...]
    x_hi = xn.astype(BF16)
    x_lo = (xn - x_hi.astype(F32)).astype(BF16)
    w_hi = w.astype(BF16)
    w_lo = (w - w_hi.astype(F32)).astype(BF16)
    acc = jnp.dot(x_hi, w_hi, preferred_element_type=F32)
    acc = acc + jnp.dot(x_lo, w_hi, preferred_element_type=F32) + jnp.dot(x_hi, w_lo, preferred_element_type=F32)
    o_ref[...] = acc + b_ref[...]


def router_logits(h, gain, router_w, router_b, *, tm=512):
    rows, d = h.shape
    tm = min(tm, rows)
    n_e = router_w.shape[1]
    w = jnp.pad(router_w, ((0, 0), (0, LANES - n_e)))
    b = jnp.pad(router_b.astype(F32), (0, LANES - n_e)).reshape(1, LANES)
    return pl.pallas_call(
        _router_kernel,
        grid=(rows // tm,),
        in_specs=[pl.BlockSpec((tm, d), lambda i: (i, 0)), pl.BlockSpec((1, d), lambda i: (0, 0)),
                  pl.BlockSpec((d, LANES), lambda i: (0, 0)), pl.BlockSpec((1, LANES), lambda i: (0, 0))],
        out_specs=pl.BlockSpec((tm, LANES), lambda i: (i, 0)),
        out_shape=jax.ShapeDtypeStruct((rows, LANES), F32),
        compiler_params=_cparams(1),
        name="moe_router",
    )(h, gain.reshape(1, d).astype(F32), w, b)


def _row_copy(src_hbm, dst, sem, src_row, dst_row):
    return pltpu.make_async_copy(src_hbm.at[pl.ds(src_row, 1)], dst.at[pl.ds(dst_row, 1)], sem)


def _gather_norm_kernel(tok_ref, h_hbm, g_ref, o_ref, buf, sem, *, tm, nblk):
    rb = pl.program_id(0)

    def issue(blk, slot):
        def group(g, carry):
            for u in range(ROWS_PER_ISSUE):
                r = g * ROWS_PER_ISSUE + u
                _row_copy(h_hbm, buf.at[slot], sem.at[slot], tok_ref[blk * tm + r], r).start(priority=u % 2)
            return carry
        lax.fori_loop(0, tm // ROWS_PER_ISSUE, group, 0)

    @pl.when(rb == 0)
    def _():
        issue(0, 0)

    @pl.when(rb + 1 < nblk)
    def _():
        issue(rb + 1, (rb + 1) % 2)

    slot = rb % 2
    pltpu.make_async_copy(h_hbm.at[pl.ds(0, tm)], buf.at[slot], sem.at[slot]).wait()
    xf = buf[slot]
    var = jnp.mean(xf * xf, axis=-1, keepdims=True)
    o_ref[...] = (xf * lax.rsqrt(var + RMS_EPS) * g_ref[...]).astype(o_ref.dtype)


def gather_norm(h, gain, slot_tok, *, tm):
    rows, d = h.shape
    n_slots = slot_tok.shape[0]
    nblk = n_slots // tm
    kern = functools.partial(_gather_norm_kernel, tm=tm, nblk=nblk)
    return pl.pallas_call(
        kern,
        grid_spec=pltpu.PrefetchScalarGridSpec(
            num_scalar_prefetch=1,
            grid=(nblk,),
            in_specs=[pl.BlockSpec(memory_space=pl.ANY), pl.BlockSpec((1, d), lambda i, t: (0, 0))],
            out_specs=pl.BlockSpec((tm, d), lambda i, t: (i, 0)),
            scratch_shapes=[pltpu.VMEM((2, tm, d), F32), pltpu.SemaphoreType.DMA((2,))],
        ),
        out_shape=jax.ShapeDtypeStruct((n_slots, d), BF16),
        compiler_params=_cparams(1),
        name="moe_gather_norm",
    )(slot_tok, h, gain.reshape(1, d).astype(F32))


def _combine_kernel(slot_ref, y_hbm, h_ref, gate_ref, g_ref, o_ref, buf, sem, *, tc, nblk):
    i = pl.program_id(0)

    def issue(blk, slot):
        def group(g, carry):
            for u in range(ROWS_PER_ISSUE):
                r = g * ROWS_PER_ISSUE + u
                for kk in range(TOP_K):
                    src = slot_ref[(blk * tc + r) * TOP_K + kk]
                    _row_copy(y_hbm, buf.at[slot, kk], sem.at[slot], src, r).start(priority=kk % 2)
            return carry
        lax.fori_loop(0, tc // ROWS_PER_ISSUE, group, 0)

    @pl.when(i == 0)
    def _():
        issue(0, 0)

    @pl.when(i + 1 < nblk)
    def _():
        issue(i + 1, (i + 1) % 2)

    slot = i % 2
    for kk in range(TOP_K):
        pltpu.make_async_copy(y_hbm.at[pl.ds(0, tc)], buf.at[slot, kk], sem.at[slot]).wait()
    gate = gate_ref[...]
    xf = h_ref[...] + gate[:, 0:1] * buf[slot, 0] + gate[:, 1:2] * buf[slot, 1]
    var = jnp.mean(xf * xf, axis=-1, keepdims=True)
    o_ref[...] = xf * lax.rsqrt(var + RMS_EPS) * g_ref[...]


def combine_norm(h, y, tok_slots, gates, final_gain, *, tc=256):
    rows, d = h.shape
    tc = min(tc, rows)
    nblk = rows // tc
    kern = functools.partial(_combine_kernel, tc=tc, nblk=nblk)
    return pl.pallas_call(
        kern,
        grid_spec=pltpu.PrefetchScalarGridSpec(
            num_scalar_prefetch=1,
            grid=(nblk,),
            in_specs=[pl.BlockSpec(memory_space=pl.ANY), pl.BlockSpec((tc, d), lambda i, s: (i, 0)),
                      pl.BlockSpec((tc, TOP_K), lambda i, s: (i, 0)), pl.BlockSpec((1, d), lambda i, s: (0, 0))],
            out_specs=pl.BlockSpec((tc, d), lambda i, s: (i, 0)),
            scratch_shapes=[pltpu.VMEM((2, TOP_K, tc, d), F32), pltpu.SemaphoreType.DMA((2,))],
        ),
        out_shape=jax.ShapeDtypeStruct((rows, d), F32),
        compiler_params=_cparams(1),
        name="moe_combine_norm",
    )(tok_slots, y, h, gates, final_gain.reshape(1, d).astype(F32))


def _route(logits, n_experts, tm):
    n_tok = logits.shape[0]
    n_asg = n_tok * TOP_K
    nblk = n_asg // tm + n_experts
    top_logit, top_idx = lax.top_k(logits[:, :n_experts], TOP_K)
    gates = jax.nn.softmax(top_logit, axis=-1)
    flat_e = top_idx.reshape(n_asg).astype(jnp.int32)
    onehot = (flat_e[:, None] == jnp.arange(n_experts, dtype=jnp.int32)[None, :]).astype(jnp.int32)
    csum = jnp.cumsum(onehot, axis=0)
    rank = jnp.sum(csum * onehot, axis=1) - 1
    counts = csum[-1]
    padded = (counts + tm - 1) // tm * tm
    pad_end = jnp.cumsum(padded)
    pad_start = pad_end - padded
    dest = (pad_start[flat_e] + rank).astype(jnp.int32)
    slot_tok = jnp.zeros((nblk * tm,), jnp.int32).at[dest].set(jnp.arange(n_asg, dtype=jnp.int32) // TOP_K)
    nused = (pad_end[-1] // tm).astype(jnp.int32)
    blk = jnp.arange(nblk, dtype=jnp.int32)
    be = jnp.minimum(jnp.searchsorted(pad_end, blk * tm, side="right"), n_experts - 1).astype(jnp.int32)
    be = jnp.where(blk < nused, be, be[jnp.maximum(nused - 1, 0)])
    seg_end = pad_end[be] // tm
    nxt = jnp.where(seg_end < nused, be[jnp.minimum(seg_end, nblk - 1)], -1).astype(jnp.int32)
    return slot_tok, gates, dest, be, nused.reshape(1), nxt


def _rope_table(positions):
    inv_freq = ROPE_THETA ** (-jnp.arange(0, QK_ROPE, 2, dtype=F32) / QK_ROPE)
    ang = positions.astype(F32)[..., None] * inv_freq
    cos, sin = jnp.cos(ang), jnp.sin(ang)
    return jnp.concatenate([cos, cos, sin, sin], axis=-1).reshape(-1, 2 * QK_ROPE)


def _rot_cols(w):
    half = w.shape[-1] // 2
    return jnp.concatenate([-w[..., half:], w[..., :half]], axis=-1)


def _mla_layer(h, cs, ln, w_in, q_norm, w_qb, kv_norm, w_kvb, w_o, *, batch, seq):
    d = h.shape[1]
    q_lora = q_norm.shape[0]
    kv_lora = kv_norm.shape[0]
    heads = w_o.shape[0] // V_HEAD
    assert q_lora == kv_lora and q_lora % LANES == 0
    w_rope = w_in[:, q_lora + kv_lora:]
    w_in_p = jnp.concatenate([w_in, _rot_cols(w_rope)], axis=1)
    n_in = w_in_p.shape[1]
    wq = w_qb.reshape(q_lora, heads, QK_NOPE + QK_ROPE)
    wq_p = jnp.concatenate([wq, _rot_cols(wq[..., QK_NOPE:])], axis=-1).reshape(q_lora, heads * Q_HEAD_PAD)

    proj = gmm(h, [w_in_p[None]], n_out=n_in, tm=512, tn=n_in, out_dtype=F32, gain=ln, name="mla_in_proj")
    scale = LOG2_E * (QK_NOPE + QK_ROPE) ** -0.5
    q = gmm(proj, [wq_p[None]], n_out=heads * Q_HEAD_PAD, tm=2048, tn=1024, out_dtype=BF16, mode="rope", cs=cs,
            scale=scale, gain=q_norm, x_col_blk=0, name="mla_q_proj")
    kv = gmm(proj, [w_kvb[None]], n_out=heads * (QK_NOPE + V_HEAD), tm=2048, tn=1024, out_dtype=BF16,
             gain=kv_norm, x_col_blk=1, name="mla_kv_proj")
    kpe = rope_shared_key(proj, cs, (q_lora + kv_lora) // LANES)
    o = mla_attention_core(q, kv, kpe, batch=batch, seq=seq, heads=heads)
    return gmm(o, [w_o[None]], n_out=d, tm=1024, tn=1024, out_dtype=F32, residual=h, name="mla_out_proj")


def _ffn_layer(h, ln, w_gate, w_up, w_down):
    d = h.shape[1]
    f = w_gate.shape[1]
    hn = rmsnorm(h, ln)
    mid = gmm(hn, [w_gate[None], w_up[None]], n_out=f, tm=1024, tn=1024 if f % 1024 == 0 else f // 7,
              out_dtype=BF16, mode="swiglu", name="ffn_up")
    return gmm(mid, [w_down[None]], n_out=d, tm=512, tn=512, out_dtype=F32, residual=h, name="ffn_down")


def _gdn_layer(h, ln, w_in, conv_w, a_log, dt_bias, norm_w, w_o, *, batch, seq):
    d = h.shape[1]
    heads = a_log.shape[0]
    n_main = 4 * heads * LANES
    hn = rmsnorm(h, ln)
    proj = gmm(hn, [w_in[None]], n_out=n_main, tm=1024, tn=1024, out_dtype=F32, name="gdn_in_proj")
    gates = gdn_gates(hn, w_in[:, n_main:], a_log, dt_bias, heads=heads)
    o = gdn_delta(proj, conv_w, gates, norm_w, batch=batch, seq=seq, heads=heads)
    return gmm(o, [w_o[None]], n_out=d, tm=1024, tn=1024, out_dtype=F32, residual=h, name="gdn_out_proj")


def _moe_layer_final(h, ln, router_w, router_b, w_gate, w_up, w_down, final_gain, *, tm=512):
    rows, d = h.shape
    n_e, _, f = w_gate.shape
    tm = min(tm, rows // 4)
    logits = router_logits(h, ln, router_w, router_b)
    slot_tok, gates, dest, be, nused, nxt = _route(logits, n_e, tm)
    xs = gather_norm(h, ln, slot_tok, tm=tm)
    mid = gmm(xs, [w_gate, w_up], n_out=f, tm=tm, tn=1024 if f % 1024 == 0 else f // 7, out_dtype=BF16,
              be=be, nused=nused, nxt=nxt, mode="swiglu", name="moe_up")
    y = gmm(mid, [w_down], n_out=d, tm=tm, tn=512, out_dtype=F32, be=be, nused=nused, nxt=nxt, name="moe_down")
    return combine_norm(h, y, dest, gates, final_gain)


def kernel(x, positions, ln_mix_mla, mla_w_in, mla_q_norm, mla_w_qb, mla_kv_norm, mla_w_kvb, mla_w_o, ln_ffn_dense, ffn_w_gate, ffn_w_up, ffn_w_down, ln_mix_gdn, gdn_w_in, gdn_conv_w, gdn_a_log, gdn_dt_bias, gdn_norm, gdn_w_o, ln_ffn_moe, moe_router, moe_router_bias, moe_w_gate, moe_w_up, moe_w_down, final_norm):
    batch, seq, d = x.shape
    assert ln_mix_mla.shape[0] == 1 and ln_mix_gdn.shape[0] == 1, "two-layer trunk: one MLA and one DeltaNet layer"
    h = x.reshape(batch * seq, d)
    cs = _rope_table(positions)
    h = _mla_layer(h, cs, ln_mix_mla[0], mla_w_in[0], mla_q_norm[0], mla_w_qb[0], mla_kv_norm[0], mla_w_kvb[0],
                   mla_w_o[0], batch=batch, seq=seq)
    h = _ffn_layer(h, ln_ffn_dense[0], ffn_w_gate[0], ffn_w_up[0], ffn_w_down[0])
    h = _gdn_layer(h, ln_mix_gdn[0], gdn_w_in[0], gdn_conv_w[0], gdn_a_log[0], gdn_dt_bias[0], gdn_norm[0],
                   gdn_w_o[0], batch=batch, seq=seq)
    out = _moe_layer_final(h, ln_ffn_moe[0], moe_router[0], moe_router_bias[0], moe_w_gate[0], moe_w_up[0],
                           moe_w_down[0], final_norm)
    return out.reshape(batch, seq, d)
```

```python
import functools

import jax
import jax.numpy as jnp
from jax import lax
from jax.experimental import pallas as pl
from jax.experimental.pallas import tpu as pltpu

F32 = jnp.float32
BF16 = jnp.bfloat16

RMS_EPS = 1e-6
ROPE_THETA = 10000.0
QK_NOPE = 128
QK_ROPE = 64
V_HEAD = 128
Q_HEAD_PAD = 256
GDN_DK = 128
GDN_DV = 128
CHUNK = 64
TOP_K = 2
LANES = 128
VMEM_LIMIT = 56 * 1024 * 1024
NEG_BIG = -1e30
ROWS_PER_ISSUE = 8
LOG2_E = 1.4426950408889634


def _cparams(n_axes):
    return pltpu.CompilerParams(dimension_semantics=("arbitrary",) * n_axes, vmem_limit_bytes=VMEM_LIMIT)


def _silu(x):
    return x * jax.nn.sigmoid(x)


def _rmsnorm_kernel(x_ref, g_ref, o_ref):
    xf = x_ref[...].astype(F32)
    var = jnp.mean(xf * xf, axis=-1, keepdims=True)
    o_ref[...] = (xf * lax.rsqrt(var + RMS_EPS) * g_ref[...]).astype(o_ref.dtype)


def rmsnorm(x, gain, *, tm=512, out_dtype=BF16):
    rows = x.shape[0]
    k = gain.shape[-1]
    tm = min(tm, rows)
    return pl.pallas_call(
        _rmsnorm_kernel,
        grid=(rows // tm,),
        in_specs=[pl.BlockSpec((tm, k), lambda i: (i, 0)), pl.BlockSpec((1, k), lambda i: (0, 0))],
        out_specs=pl.BlockSpec((tm, k), lambda i: (i, 0)),
        out_shape=jax.ShapeDtypeStruct((rows, k), out_dtype),
        compiler_params=_cparams(1),
        name="rmsnorm",
    )(x, gain.reshape(1, k).astype(F32))


def _gmm_kernel(be_ref, nu_ref, nxt_ref, x_ref, *rest, n_w, n_tiles, tn, mode, has_gain, has_res, scale):
    gain_ref = res_ref = cs_ref = None
    if has_gain:
        gain_ref, rest = rest[0], rest[1:]
    w_hbm = rest[:n_w]
    rest = rest[n_w:]
    if has_res:
        res_ref, rest = rest[0], rest[1:]
    if mode == "rope":
        cs_ref, rest = rest[0], rest[1:]
    o_ref, stage, wbf, sem = rest
    n = pl.program_id(0)
    rb = pl.program_id(1)

    def weight_copy(i, expert, n_tile):
        col = pl.multiple_of(n_tile * tn, tn)
        return pltpu.make_async_copy(w_hbm[i].at[expert, :, pl.ds(col, tn)], stage.at[i], sem.at[i])

    def start_weights(expert, n_tile):
        for i in range(n_w):
            weight_copy(i, expert, n_tile).start()

    prev = be_ref[jnp.maximum(rb - 1, 0)]
    changed = jnp.logical_or(rb == 0, be_ref[rb] != prev)

    @pl.when(changed)
    def _():
        @pl.when(jnp.logical_and(n == 0, rb == 0))
        def _():
            start_weights(be_ref[0], 0)

        for i in range(n_w):
            weight_copy(i, 0, 0).wait()
            wbf[i] = stage[i].astype(BF16)
        nxt = nxt_ref[rb]

        @pl.when(nxt >= 0)
        def _():
            start_weights(nxt, n)

        @pl.when(jnp.logical_and(nxt < 0, n + 1 < n_tiles))
        def _():
            start_weights(be_ref[0], n + 1)

    @pl.when(rb < nu_ref[0])
    def _():
        x = x_ref[...]
        if has_gain:
            var = jnp.mean(x * x, axis=-1, keepdims=True)
            x = (x * lax.rsqrt(var + RMS_EPS) * gain_ref[...]).astype(BF16)
        if mode == "swiglu":
            a = jnp.dot(x, wbf[0], preferred_element_type=F32)
            b = jnp.dot(x, wbf[1], preferred_element_type=F32)
            o_ref[...] = (_silu(a) * b).astype(o_ref.dtype)
        else:
            acc = jnp.dot(x, wbf[0], preferred_element_type=F32)
            if has_res:
                acc = acc + res_ref[...]
            if mode == "rope":
                cs = cs_ref[...]
                for hh in range(acc.shape[1] // Q_HEAD_PAD):
                    c0 = hh * Q_HEAD_PAD
                    o_ref[:, c0:c0 + QK_NOPE] = (acc[:, c0:c0 + QK_NOPE] * scale).astype(o_ref.dtype)
                    g = acc[:, c0 + QK_NOPE:c0 + Q_HEAD_PAD] * cs
                    r = g + pltpu.roll(g, QK_ROPE, axis=1)
                    o_ref[:, c0 + QK_NOPE:c0 + Q_HEAD_PAD] = (r * scale).astype(o_ref.dtype)
            else:
                o_ref[...] = acc.astype(o_ref.dtype)

    @pl.when(rb >= nu_ref[0])
    def _():
        o_ref[...] = jnp.zeros(o_ref.shape, o_ref.dtype)


def gmm(x, ws, *, n_out, tm, tn, out_dtype, be=None, nused=None, nxt=None, mode="plain", gain=None, x_col_blk=0,
        residual=None, cs=None, scale=None, name="gmm"):
    rows = x.shape[0]
    k = ws[0].shape[1]
    tm = min(tm, rows)
    tn = min(tn, n_out)
    nblk = rows // tm
    n_tiles = n_out // tn
    if be is None:
        be = jnp.zeros((nblk,), jnp.int32)
        nused = jnp.full((1,), nblk, jnp.int32)
        nxt = jnp.full((nblk,), -1, jnp.int32)
    n_w = len(ws)

    def x_map(n, rb, be_r, nu_r, nx_r):
        return (jnp.minimum(rb, nu_r[0] - 1), x_col_blk)

    def row_map(n, rb, be_r, nu_r, nx_r):
        return (rb, 0)

    def out_map(n, rb, be_r, nu_r, nx_r):
        return (rb, n)

    in_specs = [pl.BlockSpec((tm, k), x_map)]
    args = [x]
    if gain is not None:
        in_specs.append(pl.BlockSpec((1, k), lambda n, rb, be_r, nu_r, nx_r: (0, 0)))
        args.append(gain.reshape(1, k).astype(F32))
    in_specs += [pl.BlockSpec(memory_space=pl.ANY) for _ in ws]
    args += list(ws)
    if residual is not None:
        in_specs.append(pl.BlockSpec((tm, tn), out_map))
        args.append(residual)
    if mode == "rope":
        in_specs.append(pl.BlockSpec((tm, LANES), row_map))
        args.append(cs)
    kern = functools.partial(_gmm_kernel, n_w=n_w, n_tiles=n_tiles, tn=tn, mode=mode, has_gain=gain is not None,
                             has_res=residual is not None, scale=scale)
    return pl.pallas_call(
        kern,
        grid_spec=pltpu.PrefetchScalarGridSpec(
            num_scalar_prefetch=3,
            grid=(n_tiles, nblk),
            in_specs=in_specs,
            out_specs=pl.BlockSpec((tm, tn), out_map),
            scratch_shapes=[pltpu.VMEM((n_w, k, tn), F32), pltpu.VMEM((n_w, k, tn), BF16),
                            pltpu.SemaphoreType.DMA((n_w,))],
        ),
        out_shape=jax.ShapeDtypeStruct((rows, n_out), out_dtype),
        compiler_params=_cparams(2),
        name=name,
    )(be, nused, nxt, *args)


def _kpe_kernel(x_ref, cs_ref, o_ref):
    g = x_ref[...] * cs_ref[...]
    r = g + pltpu.roll(g, QK_ROPE, axis=1)
    lane = lax.broadcasted_iota(jnp.int32, r.shape, 1)
    o_ref[...] = jnp.where(lane < QK_ROPE, r, 0.0).astype(o_ref.dtype)


def rope_shared_key(proj, cs, col_blk, *, tm=512):
    rows = proj.shape[0]
    tm = min(tm, rows)
    return pl.pallas_call(
        _kpe_kernel,
        grid=(rows // tm,),
        in_specs=[pl.BlockSpec((tm, LANES), lambda i: (i, col_blk)), pl.BlockSpec((tm, LANES), lambda i: (i, 0))],
        out_specs=pl.BlockSpec((tm, LANES), lambda i: (i, 0)),
        out_shape=jax.ShapeDtypeStruct((rows, LANES), BF16),
        compiler_params=_cparams(1),
        name="rope_shared_key",
    )(proj, cs)


def _attn_kernel(q_ref, kv_ref, kpe_ref, o_ref, *, tq, tk, hp):
    qi = pl.program_id(2)
    n_full = (qi * tq) // tk
    n_diag = max(1, tq // tk)
    row = lax.broadcasted_iota(jnp.int32, (tq, tk), 0) + qi * tq
    col0 = lax.broadcasted_iota(jnp.int32, (tq, tk), 1)
    qs = [q_ref[:, a * Q_HEAD_PAD:(a + 1) * Q_HEAD_PAD] for a in range(hp)]

    def step(j, carry, masked):
        r0 = pl.multiple_of(j * tk, tk)
        kpe = kpe_ref[pl.ds(r0, tk), :]
        out = []
        for a in range(hp):
            m, l, acc = carry[a]
            c0 = a * (QK_NOPE + V_HEAD)
            k = jnp.concatenate([kv_ref[pl.ds(r0, tk), c0:c0 + QK_NOPE], kpe], axis=1)
            v = kv_ref[pl.ds(r0, tk), c0 + QK_NOPE:c0 + QK_NOPE + V_HEAD]
            s = lax.dot_general(qs[a], k, (((1,), (1,)), ((), ())), preferred_element_type=F32)
            if masked:
                s = jnp.where(col0 + j * tk <= row, s, NEG_BIG)
            m_new = jnp.maximum(m, jnp.max(s, axis=1, keepdims=True))
            p = jnp.exp2(s - m_new)
            alpha = jnp.exp2(m - m_new)
            l = alpha * l + jnp.sum(p, axis=1, keepdims=True)
            acc = alpha * acc + jnp.dot(p.astype(BF16), v, preferred_element_type=F32)
            out.append((m_new, l, acc))
        return tuple(out)

    init = tuple((jnp.full((tq, 1), NEG_BIG, F32), jnp.zeros((tq, 1), F32), jnp.zeros((tq, V_HEAD), F32))
                 for _ in range(hp))
    carry = lax.fori_loop(0, n_full, lambda j, c: step(j, c, False), init)
    for jd in range(n_diag):
        carry = step(n_full + jd, carry, True)
    for a in range(hp):
        _, l, acc = carry[a]
        o_ref[:, a * V_HEAD:(a + 1) * V_HEAD] = (acc / l).astype(o_ref.dtype)


def mla_attention_core(q, kv, kpe, *, batch, seq, heads, tq=512, tk=512, hp=4):
    tq = min(tq, seq)
    tk = min(tk, seq)
    hp = min(hp, heads)
    assert (tk % tq == 0 or tq % tk == 0) and seq % tk == 0
    nq = seq // tq
    kern = functools.partial(_attn_kernel, tq=tq, tk=tk, hp=hp)
    return pl.pallas_call(
        kern,
        grid=(batch, heads // hp, nq),
        in_specs=[
            pl.BlockSpec((tq, hp * Q_HEAD_PAD), lambda b, h, i: (b * nq + i, h)),
            pl.BlockSpec((seq, hp * (QK_NOPE + V_HEAD)), lambda b, h, i: (b, h)),
            pl.BlockSpec((seq, LANES), lambda b, h, i: (b, 0)),
        ],
        out_specs=pl.BlockSpec((tq, hp * V_HEAD), lambda b, h, i: (b * nq + i, h)),
        out_shape=jax.ShapeDtypeStruct((batch * seq, heads * V_HEAD), BF16),
        compiler_params=_cparams(3),
        name="mla_flash_attention",
    )(q, kv, kpe)


def _gdn_gate_kernel(x_ref, w_ref, alog_ref, dtb_ref, o_ref, *, heads):
    ba = jnp.dot(x_ref[...], w_ref[...].astype(BF16), preferred_element_type=F32)
    lane = lax.broadcasted_iota(jnp.int32, ba.shape, 1)
    beta = jax.nn.sigmoid(ba)
    z = ba + dtb_ref[...]
    softplus = jnp.maximum(z, 0.0) + jnp.log1p(jnp.exp(-jnp.abs(z)))
    g = -jnp.exp(alog_ref[...]) * softplus
    o_ref[...] = jnp.where(lane < heads, beta, g)


def gdn_gates(xn, w_ba, a_log, dt_bias, *, heads, tm=512):
    rows, k = xn.shape
    tm = min(tm, rows)
    pad = LANES - 2 * heads
    w = jnp.pad(w_ba, ((0, 0), (0, pad)))
    alog = jnp.pad(a_log.astype(F32), (heads, pad)).reshape(1, LANES)
    dtb = jnp.pad(dt_bias.astype(F32), (heads, pad)).reshape(1, LANES)
    return pl.pallas_call(
        functools.partial(_gdn_gate_kernel, heads=heads),
        grid=(rows // tm,),
        in_specs=[pl.BlockSpec((tm, k), lambda i: (i, 0)), pl.BlockSpec((k, LANES), lambda i: (0, 0)),
                  pl.BlockSpec((1, LANES), lambda i: (0, 0)), pl.BlockSpec((1, LANES), lambda i: (0, 0))],
        out_specs=pl.BlockSpec((tm, LANES), lambda i: (i, 0)),
        out_shape=jax.ShapeDtypeStruct((rows, LANES), F32),
        compiler_params=_cparams(1),
        name="gdn_gates",
    )(xn, w, alog, dtb)


def _bmm(a, b):
    return lax.dot_general(a, b, (((2,), (1,)), ((0,), (0,))), preferred_element_type=F32)


def _bmm_nt(a, b):
    return lax.dot_general(a, b, (((2,), (2,)), ((0,), (0,))), preferred_element_type=F32)


def _bmm_tn(a, b):
    return lax.dot_general(a, b, (((1,), (1,)), ((0,), (0,))), preferred_element_type=F32)


def _gdn_kernel(q_ref, k_ref, v_ref, z_ref, cwq_ref, cwk_ref, cwv_ref, gate_ref, nw_ref, o_ref,
                halo, xx, lhs_s, bt_s, au_s, egl_s, st, *, hg, ts, heads):
    nc = ts // CHUNK
    cb = min(16, nc)
    hgi = pl.program_id(1)
    first = pl.program_id(2) == 0

    lane_t = lax.broadcasted_iota(jnp.int32, (ts, LANES), 1)
    ii = lax.broadcasted_iota(jnp.int32, (CHUNK, CHUNK), 0)
    jj = lax.broadcasted_iota(jnp.int32, (CHUNK, CHUNK), 1)
    incl = ii >= jj
    strict = ii > jj
    lower_ones = jnp.where(incl, 1.0, 0.0).astype(F32)
    eye = jnp.where(ii == jj, 1.0, 0.0).astype(F32)
    gates = gate_ref[...]

    def conv_silu(idx, x_ref, w_ref, c0):
        x = x_ref[:, c0:c0 + LANES]
        w = w_ref[:, c0:c0 + LANES]
        xb = xx.at[idx * hg + c0 // LANES]
        xb[0:8, :] = jnp.where(first, 0.0, halo[idx, :, c0:c0 + LANES])
        xb[8:ts + 8, :] = x
        halo[idx, :, c0:c0 + LANES] = x[ts - 8:ts, :]
        y = x * w[3:4, :]
        for s in (1, 2, 3):
            y = y + xb[8 - s:8 - s + ts, :] * w[3 - s:4 - s, :]
        return _silu(y)

    @pl.when(first)
    def _():
        st[...] = jnp.zeros(st.shape, F32)

    for hh in range(hg):
        c0 = hh * LANES
        head = hgi * hg + hh
        q = conv_silu(0, q_ref, cwq_ref, c0)
        q = q * lax.rsqrt(jnp.sum(q * q, axis=-1, keepdims=True) + RMS_EPS) * (GDN_DK ** -0.5)
        k = conv_silu(1, k_ref, cwk_ref, c0)
        k = k * lax.rsqrt(jnp.sum(k * k, axis=-1, keepdims=True) + RMS_EPS)
        v = conv_silu(2, v_ref, cwv_ref, c0)

        b_col = jnp.sum(jnp.where(lane_t == head, gates, 0.0), axis=1, keepdims=True)
        g_col = jnp.sum(jnp.where(lane_t == heads + head, gates, 0.0), axis=1, keepdims=True)
        for sb in range(nc // cb):
            r0, r1 = sb * cb * CHUNK, (sb + 1) * cb * CHUNK
            cs0, cs1 = sb * cb, (sb + 1) * cb
            b3 = b_col[r0:r1].reshape(cb, CHUNK, 1)
            g3 = g_col[r0:r1].reshape(cb, CHUNK, 1)
            g_row3 = jnp.sum(g3 * eye, axis=1, keepdims=True)
            gc3 = jnp.sum(lower_ones * g_row3, axis=2, keepdims=True)
            gc_row3 = jnp.sum(gc3 * eye, axis=1, keepdims=True)
            gl3 = gc3[:, CHUNK - 1:CHUNK, :]
            decay = jnp.where(incl, jnp.exp(jnp.where(incl, gc3 - gc_row3, 0.0)), 0.0)
            egc3 = jnp.exp(gc3)

            q3 = q[r0:r1].reshape(cb, CHUNK, LANES)
            k3 = k[r0:r1].reshape(cb, CHUNK, LANES)
            v3 = v[r0:r1].reshape(cb, CHUNK, LANES)
            q16 = q3.astype(BF16)
            k16 = k3.astype(BF16)
            lower = jnp.where(strict, _bmm_nt(k16, k16) * decay * b3, 0.0)
            p = -lower
            t_inv = eye + p
            for _ in range(5):
                p16 = p.astype(BF16)
                p = _bmm(p16, p16)
                t_inv = t_inv + _bmm(t_inv.astype(BF16), p.astype(BF16))
            vk16 = jnp.concatenate([(v3 * b3).astype(BF16), (k3 * (b3 * egc3)).astype(BF16)], axis=-1)
            uw16 = _bmm(t_inv.astype(BF16), vk16).astype(BF16)
            a16 = jnp.where(incl, _bmm_nt(q16, k16) * decay, 0.0).astype(BF16)
            awu = _bmm(a16, uw16)
            kd16 = (k3 * jnp.exp(gl3 - gc3)).astype(BF16)
            mb = _bmm_tn(kd16, uw16)
            lhs_s[hh, cs0:cs1, 0:GDN_DK, :] = mb[:, :, LANES:].astype(BF16)
            lhs_s[hh, cs0:cs1, GDN_DK:GDN_DK + CHUNK, :] = (q3 * egc3 - awu[:, :, LANES:]).astype(BF16)
            bt_s[hh, cs0:cs1] = mb[:, :, 0:LANES]
            au_s[hh, cs0:cs1] = awu[:, :, 0:LANES]
            egl_s[hh, cs0:cs1] = jnp.broadcast_to(jnp.exp(gl3), (cb, 1, LANES))

    def chunk_body(c, carry):
        s_old = [st[hh] for hh in range(hg)]
        res = [jnp.dot(lhs_s[hh, c], s_old[hh].astype(BF16), preferred_element_type=F32) for hh in range(hg)]
        for hh in range(hg):
            st[hh] = egl_s[hh, c] * s_old[hh] - res[hh][0:GDN_DK, :] + bt_s[hh, c]
            au_s[hh, c] = res[hh][GDN_DK:GDN_DK + CHUNK, :] + au_s[hh, c]
        return carry

    lax.fori_loop(0, nc, chunk_body, 0)

    nw = nw_ref[...]
    for hh in range(hg):
        c0 = hh * LANES
        o = au_s[hh].reshape(ts, LANES)
        on = o * lax.rsqrt(jnp.mean(o * o, axis=-1, keepdims=True) + RMS_EPS) * nw
        o_ref[:, c0:c0 + LANES] = (on * _silu(z_ref[:, c0:c0 + LANES])).astype(o_ref.dtype)


def gdn_delta(proj, conv_w, gates, norm_w, *, batch, seq, heads, hg=4, ts=1024):
    hg = min(hg, heads)
    ts = min(ts, seq)
    nhb = heads // hg
    nst = seq // ts
    wblk = hg * LANES
    nc = ts // CHUNK
    kern = functools.partial(_gdn_kernel, hg=hg, ts=ts, heads=heads)

    def pmap(off):
        return lambda b, h, i: (b * nst + i, off * nhb + h)

    def cmap(off):
        return lambda b, h, i: (0, off * nhb + h)

    return pl.pallas_call(
        kern,
        grid=(batch, nhb, nst),
        in_specs=[pl.BlockSpec((ts, wblk), pmap(0)), pl.BlockSpec((ts, wblk), pmap(1)),
                  pl.BlockSpec((ts, wblk), pmap(2)), pl.BlockSpec((ts, wblk), pmap(3)),
                  pl.BlockSpec((4, wblk), cmap(0)), pl.BlockSpec((4, wblk), cmap(1)), pl.BlockSpec((4, wblk), cmap(2)),
                  pl.BlockSpec((ts, LANES), lambda b, h, i: (b * nst + i, 0)),
                  pl.BlockSpec((1, LANES), lambda b, h, i: (0, 0))],
        out_specs=pl.BlockSpec((ts, wblk), lambda b, h, i: (b * nst + i, h)),
        out_shape=jax.ShapeDtypeStruct((batch * seq, heads * GDN_DV), BF16),
        scratch_shapes=[pltpu.VMEM((3, 8, wblk), F32),
                        pltpu.VMEM((3 * hg, ts + 8, LANES), F32),
                        pltpu.VMEM((hg, nc, GDN_DK + CHUNK, LANES), BF16),
                        pltpu.VMEM((hg, nc, GDN_DK, GDN_DV), F32),
                        pltpu.VMEM((hg, nc, CHUNK, GDN_DV), F32),
                        pltpu.VMEM((hg, nc, 1, LANES), F32),
                        pltpu.VMEM((hg, GDN_DK, GDN_DV), F32)],
        compiler_params=_cparams(3),
        name="gdn_delta_rule",
    )(proj, proj, proj, proj, conv_w, conv_w, conv_w, gates, norm_w.reshape(1, LANES).astype(F32))


def _router_kernel(x_ref, g_ref, w_ref, b_ref, o_ref, pk_ref):
    xf = x_ref[...]
    var = jnp.mean(xf * xf, axis=-1, keepdims=True)
    xn = xf * lax.rsqrt(var + RMS_EPS) * g_ref[...]
    w = w_ref[...]
    x_hi = xn.astype(BF16)
    x_lo = (xn - x_hi.astype(F32)).astype(BF16)
    w_hi = w.astype(BF16)
    w_lo = (w - w_hi.astype(F32)).astype(BF16)
    acc = jnp.dot(x_hi, w_hi, preferred_element_type=F32)
    acc = acc + jnp.dot(x_lo, w_hi, preferred_element_type=F32) + jnp.dot(x_hi, w_lo, preferred_element_type=F32)
    o_ref[...] = acc + b_ref[...]
    half = xn.shape[1] // 2
    bits = pltpu.bitcast(x_hi.astype(F32), jnp.uint32)
    pk_ref[...] = bits[:, half:] | (bits[:, :half] >> 16)


def router_logits(h, gain, router_w, router_b, *, tm=512):
    rows, d = h.shape
    tm = min(tm, rows)
    n_e = router_w.shape[1]
    w = jnp.pad(router_w, ((0, 0), (0, LANES - n_e)))
    b = jnp.pad(router_b.astype(F32), (0, LANES - n_e)).reshape(1, LANES)
    return pl.pallas_call(
        _router_kernel,
        grid=(rows // tm,),
        in_specs=[pl.BlockSpec((tm, d), lambda i: (i, 0)), pl.BlockSpec((1, d), lambda i: (0, 0)),
                  pl.BlockSpec((d, LANES), lambda i: (0, 0)), pl.BlockSpec((1, LANES), lambda i: (0, 0))],
        out_specs=[pl.BlockSpec((tm, LANES), lambda i: (i, 0)), pl.BlockSpec((tm, d // 2), lambda i: (i, 0))],
        out_shape=[jax.ShapeDtypeStruct((rows, LANES), F32), jax.ShapeDtypeStruct((rows, d // 2), jnp.uint32)],
        compiler_params=_cparams(1),
        name="moe_router",
    )(h, gain.reshape(1, d).astype(F32), w, b)


def _row_copy(src_hbm, dst, sem, src_row, dst_row):
    return pltpu.make_async_copy(src_hbm.at[pl.ds(src_row, 1)], dst.at[pl.ds(dst_row, 1)], sem)


def _gather_unpack_kernel(tok_ref, xp_hbm, o_ref, buf, sem, *, tm, nblk):
    rb = pl.program_id(0)

    def issue(blk, slot):
        def group(g, carry):
            for u in range(ROWS_PER_ISSUE):
                r = g * ROWS_PER_ISSUE + u
                _row_copy(xp_hbm, buf.at[slot], sem.at[slot], tok_ref[blk * tm + r], r).start(priority=u % 2)
            return carry
        lax.fori_loop(0, tm // ROWS_PER_ISSUE, group, 0)

    @pl.when(rb == 0)
    def _():
        issue(0, 0)

    @pl.when(rb + 1 < nblk)
    def _():
        issue(rb + 1, (rb + 1) % 2)

    slot = rb % 2
    pltpu.make_async_copy(xp_hbm.at[pl.ds(0, tm)], buf.at[slot], sem.at[slot]).wait()
    w = buf[slot]
    half = w.shape[1]
    o_ref[:, 0:half] = pltpu.bitcast(w << 16, F32).astype(o_ref.dtype)
    o_ref[:, half:2 * half] = pltpu.bitcast(w & jnp.uint32(0xFFFF0000), F32).astype(o_ref.dtype)


def gather_rows(xp, slot_tok, *, tm):
    rows, half = xp.shape
    n_slots = slot_tok.shape[0]
    nblk = n_slots // tm
    kern = functools.partial(_gather_unpack_kernel, tm=tm, nblk=nblk)
    return pl.pallas_call(
        kern,
        grid_spec=pltpu.PrefetchScalarGridSpec(
            num_scalar_prefetch=1,
            grid=(nblk,),
            in_specs=[pl.BlockSpec(memory_space=pl.ANY)],
            out_specs=pl.BlockSpec((tm, 2 * half), lambda i, t: (i, 0)),
            scratch_shapes=[pltpu.VMEM((2, tm, half), jnp.uint32), pltpu.SemaphoreType.DMA((2,))],
        ),
        out_shape=jax.ShapeDtypeStruct((n_slots, 2 * half), BF16),
        compiler_params=_cparams(1),
        name="moe_gather_rows",
    )(slot_tok, xp)


def _combine_kernel(slot_ref, y_hbm, h_ref, gate_ref, g_ref, o_ref, buf, sem, *, tc, nblk):
    i = pl.program_id(0)

    def issue(blk, slot):
        def group(g, carry):
            for u in range(ROWS_PER_ISSUE):
                r = g * ROWS_PER_ISSUE + u
                for kk in range(TOP_K):
                    src = slot_ref[(blk * tc + r) * TOP_K + kk]
                    _row_copy(y_hbm, buf.at[slot, kk], sem.at[slot], src, r).start(priority=kk % 2)
            return carry
        lax.fori_loop(0, tc // ROWS_PER_ISSUE, group, 0)

    @pl.when(i == 0)
    def _():
        issue(0, 0)

    @pl.when(i + 1 < nblk)
    def _():
        issue(i + 1, (i + 1) % 2)

    slot = i % 2
    for kk in range(TOP_K):
        pltpu.make_async_copy(y_hbm.at[pl.ds(0, tc)], buf.at[slot, kk], sem.at[slot]).wait()
    gate = gate_ref[...]
    xf = h_ref[...] + gate[:, 0:1] * buf[slot, 0] + gate[:, 1:2] * buf[slot, 1]
    var = jnp.mean(xf * xf, axis=-1, keepdims=True)
    o_ref[...] = xf * lax.rsqrt(var + RMS_EPS) * g_ref[...]


def combine_norm(h, y, tok_slots, gates, final_gain, *, tc=256):
    rows, d = h.shape
    tc = min(tc, rows)
    nblk = rows // tc
    kern = functools.partial(_combine_kernel, tc=tc, nblk=nblk)
    return pl.pallas_call(
        kern,
        grid_spec=pltpu.PrefetchScalarGridSpec(
            num_scalar_prefetch=1,
            grid=(nblk,),
            in_specs=[pl.BlockSpec(memory_space=pl.ANY), pl.BlockSpec((tc, d), lambda i, s: (i, 0)),
                      pl.BlockSpec((tc, TOP_K), lambda i, s: (i, 0)), pl.BlockSpec((1, d), lambda i, s: (0, 0))],
            out_specs=pl.BlockSpec((tc, d), lambda i, s: (i, 0)),
            scratch_shapes=[pltpu.VMEM((2, TOP_K, tc, d), F32), pltpu.SemaphoreType.DMA((2,))],
        ),
        out_shape=jax.ShapeDtypeStruct((rows, d), F32),
        compiler_params=_cparams(1),
        name="moe_combine_norm",
    )(tok_slots, y, h, gates, final_gain.reshape(1, d).astype(F32))


def _route(logits, n_experts, tm):
    n_tok = logits.shape[0]
    n_asg = n_tok * TOP_K
    nblk = n_asg // tm + n_experts
    top_logit, top_idx = lax.top_k(logits[:, :n_experts], TOP_K)
    gates = jax.nn.softmax(top_logit, axis=-1)
    flat_e = top_idx.reshape(n_asg).astype(jnp.int32)
    onehot = (flat_e[:, None] == jnp.arange(n_experts, dtype=jnp.int32)[None, :]).astype(jnp.int32)
    csum = jnp.cumsum(onehot, axis=0)
    rank = jnp.sum(csum * onehot, axis=1) - 1
    counts = csum[-1]
    padded = (counts + tm - 1) // tm * tm
    pad_end = jnp.cumsum(padded)
    pad_start = pad_end - padded
    dest = (pad_start[flat_e] + rank).astype(jnp.int32)
    slot_tok = jnp.zeros((nblk * tm,), jnp.int32).at[dest].set(jnp.arange(n_asg, dtype=jnp.int32) // TOP_K)
    nused = (pad_end[-1] // tm).astype(jnp.int32)
    blk = jnp.arange(nblk, dtype=jnp.int32)
    be = jnp.minimum(jnp.searchsorted(pad_end, blk * tm, side="right"), n_experts - 1).astype(jnp.int32)
    be = jnp.where(blk < nused, be, be[jnp.maximum(nused - 1, 0)])
    seg_end = pad_end[be] // tm
    nxt = jnp.where(seg_end < nused, be[jnp.minimum(seg_end, nblk - 1)], -1).astype(jnp.int32)
    return slot_tok, gates, dest, be, nused.reshape(1), nxt


def _rope_table(positions):
    inv_freq = ROPE_THETA ** (-jnp.arange(0, QK_ROPE, 2, dtype=F32) / QK_ROPE)
    ang = positions.astype(F32)[..., None] * inv_freq
    cos, sin = jnp.cos(ang), jnp.sin(ang)
    return jnp.concatenate([cos, cos, sin, sin], axis=-1).reshape(-1, 2 * QK_ROPE)


def _rot_cols(w):
    half = w.shape[-1] // 2
    return jnp.concatenate([-w[..., half:], w[..., :half]], axis=-1)


def _mla_layer(h, cs, ln, w_in, q_norm, w_qb, kv_norm, w_kvb, w_o, *, batch, seq):
    d = h.shape[1]
    q_lora = q_norm.shape[0]
    kv_lora = kv_norm.shape[0]
    heads = w_o.shape[0] // V_HEAD
    assert q_lora == kv_lora and q_lora % LANES == 0
    w_rope = w_in[:, q_lora + kv_lora:]
    w_in_p = jnp.concatenate([w_in, _rot_cols(w_rope)], axis=1)
    n_in = w_in_p.shape[1]
    wq = w_qb.reshape(q_lora, heads, QK_NOPE + QK_ROPE)
    wq_p = jnp.concatenate([wq, _rot_cols(wq[..., QK_NOPE:])], axis=-1).reshape(q_lora, heads * Q_HEAD_PAD)

    proj = gmm(h, [w_in_p[None]], n_out=n_in, tm=512, tn=n_in, out_dtype=F32, gain=ln, name="mla_in_proj")
    scale = LOG2_E * (QK_NOPE + QK_ROPE) ** -0.5
    q = gmm(proj, [wq_p[None]], n_out=heads * Q_HEAD_PAD, tm=2048, tn=1024, out_dtype=BF16, mode="rope", cs=cs,
            scale=scale, gain=q_norm, x_col_blk=0, name="mla_q_proj")
    kv = gmm(proj, [w_kvb[None]], n_out=heads * (QK_NOPE + V_HEAD), tm=2048, tn=1024, out_dtype=BF16,
             gain=kv_norm, x_col_blk=1, name="mla_kv_proj")
    kpe = rope_shared_key(proj, cs, (q_lora + kv_lora) // LANES)
    o = mla_attention_core(q, kv, kpe, batch=batch, seq=seq, heads=heads)
    return gmm(o, [w_o[None]], n_out=d, tm=1024, tn=1024, out_dtype=F32, residual=h, name="mla_out_proj")


def _ffn_layer(h, ln, w_gate, w_up, w_down):
    d = h.shape[1]
    f = w_gate.shape[1]
    hn = rmsnorm(h, ln)
    mid = gmm(hn, [w_gate[None], w_up[None]], n_out=f, tm=1024, tn=1024 if f % 1024 == 0 else f // 7,
              out_dtype=BF16, mode="swiglu", name="ffn_up")
    return gmm(mid, [w_down[None]], n_out=d, tm=512, tn=512, out_dtype=F32, residual=h, name="ffn_down")


def _gdn_layer(h, ln, w_in, conv_w, a_log, dt_bias, norm_w, w_o, *, batch, seq):
    d = h.shape[1]
    heads = a_log.shape[0]
    n_main = 4 * heads * LANES
    hn = rmsnorm(h, ln)
    proj = gmm(hn, [w_in[None]], n_out=n_main, tm=1024, tn=1024, out_dtype=F32, name="gdn_in_proj")
    gates = gdn_gates(hn, w_in[:, n_main:], a_log, dt_bias, heads=heads)
    o = gdn_delta(proj, conv_w, gates, norm_w, batch=batch, seq=seq, heads=heads)
    return gmm(o, [w_o[None]], n_out=d, tm=1024, tn=1024, out_dtype=F32, residual=h, name="gdn_out_proj")


def _moe_layer_final(h, ln, router_w, router_b, w_gate, w_up, w_down, final_gain, *, tm=512):
    rows, d = h.shape
    n_e, _, f = w_gate.shape
    tm = min(tm, rows // 4)
    logits, xn_packed = router_logits(h, ln, router_w, router_b)
    slot_tok, gates, dest, be, nused, nxt = _route(logits, n_e, tm)
    xs = gather_rows(xn_packed, slot_tok, tm=tm)
    mid = gmm(xs, [w_gate, w_up], n_out=f, tm=tm, tn=1024 if f % 1024 == 0 else f // 7, out_dtype=BF16,
              be=be, nused=nused, nxt=nxt, mode="swiglu", name="moe_up")
    y = gmm(mid, [w_down], n_out=d, tm=tm, tn=512, out_dtype=F32, be=be, nused=nused, nxt=nxt, name="moe_down")
    return combine_norm(h, y, dest, gates, final_gain)


def kernel(x, positions, ln_mix_mla, mla_w_in, mla_q_norm, mla_w_qb, mla_kv_norm, mla_w_kvb, mla_w_o, ln_ffn_dense, ffn_w_gate, ffn_w_up, ffn_w_down, ln_mix_gdn, gdn_w_in, gdn_conv_w, gdn_a_log, gdn_dt_bias, gdn_norm, gdn_w_o, ln_ffn_moe, moe_router, moe_router_bias, moe_w_gate, moe_w_up, moe_w_down, final_norm):
    batch, seq, d = x.shape
    assert ln_mix_mla.shape[0] == 1 and ln_mix_gdn.shape[0] == 1, "two-layer trunk: one MLA and one DeltaNet layer"
    h = x.reshape(batch * seq, d)
    cs = _rope_table(positions)
    h = _mla_layer(h, cs, ln_mix_mla[0], mla_w_in[0], mla_q_norm[0], mla_w_qb[0], mla_kv_norm[0], mla_w_kvb[0],
                   mla_w_o[0], batch=batch, seq=seq)
    h = _ffn_layer(h, ln_ffn_dense[0], ffn_w_gate[0], ffn_w_up[0], ffn_w_down[0])
    h = _gdn_layer(h, ln_mix_gdn[0], gdn_w_in[0], gdn_conv_w[0], gdn_a_log[0], gdn_dt_bias[0], gdn_norm[0],
                   gdn_w_o[0], batch=batch, seq=seq)
    out = _moe_layer_final(h, ln_ffn_moe[0], moe_router[0], moe_router_bias[0], moe_w_gate[0], moe_w_up[0],
                           moe_w_down[0], final_norm)
    return out.reshape(batch, seq, d)
```

```python
import functools

import jax
import jax.numpy as jnp
from jax import lax
from jax.experimental import pallas as pl
from jax.experimental.pallas import tpu as pltpu

F32 = jnp.float32
BF16 = jnp.bfloat16

RMS_EPS = 1e-6
ROPE_THETA = 10000.0
QK_NOPE = 128
QK_ROPE = 64
V_HEAD = 128
Q_HEAD_PAD = 256
GDN_DK = 128
GDN_DV = 128
CHUNK = 64
TOP_K = 2
LANES = 128
VMEM_LIMIT = 56 * 1024 * 1024
NEG_BIG = -1e30
ROWS_PER_ISSUE = 8
LOG2_E = 1.4426950408889634


def _cparams(n_axes):
    return pltpu.CompilerParams(dimension_semantics=("arbitrary",) * n_axes, vmem_limit_bytes=VMEM_LIMIT)


def _silu(x):
    return x * jax.nn.sigmoid(x)


def _rmsnorm_kernel(x_ref, g_ref, o_ref):
    xf = x_ref[...].astype(F32)
    var = jnp.mean(xf * xf, axis=-1, keepdims=True)
    o_ref[...] = (xf * lax.rsqrt(var + RMS_EPS) * g_ref[...]).astype(o_ref.dtype)


def rmsnorm(x, gain, *, tm=512, out_dtype=BF16):
    rows = x.shape[0]
    k = gain.shape[-1]
    tm = min(tm, rows)
    return pl.pallas_call(
        _rmsnorm_kernel,
        grid=(rows // tm,),
        in_specs=[pl.BlockSpec((tm, k), lambda i: (i, 0)), pl.BlockSpec((1, k), lambda i: (0, 0))],
        out_specs=pl.BlockSpec((tm, k), lambda i: (i, 0)),
        out_shape=jax.ShapeDtypeStruct((rows, k), out_dtype),
        compiler_params=_cparams(1),
        name="rmsnorm",
    )(x, gain.reshape(1, k).astype(F32))


def _gmm_kernel(be_ref, nu_ref, nxt_ref, x_ref, *rest, n_w, n_tiles, tn, mode, has_gain, has_res, scale):
    gain_ref = res_ref = cs_ref = None
    if has_gain:
        gain_ref, rest = rest[0], rest[1:]
    w_hbm = rest[:n_w]
    rest = rest[n_w:]
    if has_res:
        res_ref, rest = rest[0], rest[1:]
    if mode == "rope":
        cs_ref, rest = rest[0], rest[1:]
    o_ref, stage, wbf, sem = rest
    n = pl.program_id(0)
    rb = pl.program_id(1)

    def weight_copy(i, expert, n_tile):
        col = pl.multiple_of(n_tile * tn, tn)
        return pltpu.make_async_copy(w_hbm[i].at[expert, :, pl.ds(col, tn)], stage.at[i], sem.at[i])

    def start_weights(expert, n_tile):
        for i in range(n_w):
            weight_copy(i, expert, n_tile).start()

    prev = be_ref[jnp.maximum(rb - 1, 0)]
    changed = jnp.logical_or(rb == 0, be_ref[rb] != prev)

    @pl.when(changed)
    def _():
        @pl.when(jnp.logical_and(n == 0, rb == 0))
        def _():
            start_weights(be_ref[0], 0)

        for i in range(n_w):
            weight_copy(i, 0, 0).wait()
            wbf[i] = stage[i].astype(BF16)
        nxt = nxt_ref[rb]

        @pl.when(nxt >= 0)
        def _():
            start_weights(nxt, n)

        @pl.when(jnp.logical_and(nxt < 0, n + 1 < n_tiles))
        def _():
            start_weights(be_ref[0], n + 1)

    @pl.when(rb < nu_ref[0])
    def _():
        x = x_ref[...]
        if has_gain:
            var = jnp.mean(x * x, axis=-1, keepdims=True)
            x = (x * lax.rsqrt(var + RMS_EPS) * gain_ref[...]).astype(BF16)
        if mode == "swiglu":
            a = jnp.dot(x, wbf[0], preferred_element_type=F32)
            b = jnp.dot(x, wbf[1], preferred_element_type=F32)
            o_ref[...] = (_silu(a) * b).astype(o_ref.dtype)
        else:
            acc = jnp.dot(x, wbf[0], preferred_element_type=F32)
            if has_res:
                acc = acc + res_ref[...]
            if mode == "rope":
                cs = cs_ref[...]
                for hh in range(acc.shape[1] // Q_HEAD_PAD):
                    c0 = hh * Q_HEAD_PAD
                    o_ref[:, c0:c0 + QK_NOPE] = (acc[:, c0:c0 + QK_NOPE] * scale).astype(o_ref.dtype)
                    g = acc[:, c0 + QK_NOPE:c0 + Q_HEAD_PAD] * cs
                    r = g + pltpu.roll(g, QK_ROPE, axis=1)
                    o_ref[:, c0 + QK_NOPE:c0 + Q_HEAD_PAD] = (r * scale).astype(o_ref.dtype)
            else:
                o_ref[...] = acc.astype(o_ref.dtype)

    @pl.when(rb >= nu_ref[0])
    def _():
        o_ref[...] = jnp.zeros(o_ref.shape, o_ref.dtype)


def gmm(x, ws, *, n_out, tm, tn, out_dtype, be=None, nused=None, nxt=None, mode="plain", gain=None, x_col_blk=0,
        residual=None, cs=None, scale=None, name="gmm"):
    rows = x.shape[0]
    k = ws[0].shape[1]
    tm = min(tm, rows)
    tn = min(tn, n_out)
    nblk = rows // tm
    n_tiles = n_out // tn
    if be is None:
        be = jnp.zeros((nblk,), jnp.int32)
        nused = jnp.full((1,), nblk, jnp.int32)
        nxt = jnp.full((nblk,), -1, jnp.int32)
    n_w = len(ws)

    def x_map(n, rb, be_r, nu_r, nx_r):
        return (jnp.minimum(rb, nu_r[0] - 1), x_col_blk)

    def row_map(n, rb, be_r, nu_r, nx_r):
        return (rb, 0)

    def out_map(n, rb, be_r, nu_r, nx_r):
        return (rb, n)

    in_specs = [pl.BlockSpec((tm, k), x_map)]
    args = [x]
    if gain is not None:
        in_specs.append(pl.BlockSpec((1, k), lambda n, rb, be_r, nu_r, nx_r: (0, 0)))
        args.append(gain.reshape(1, k).astype(F32))
    in_specs += [pl.BlockSpec(memory_space=pl.ANY) for _ in ws]
    args += list(ws)
    if residual is not None:
        in_specs.append(pl.BlockSpec((tm, tn), out_map))
        args.append(residual)
    if mode == "rope":
        in_specs.append(pl.BlockSpec((tm, LANES), row_map))
        args.append(cs)
    kern = functools.partial(_gmm_kernel, n_w=n_w, n_tiles=n_tiles, tn=tn, mode=mode, has_gain=gain is not None,
                             has_res=residual is not None, scale=scale)
    return pl.pallas_call(
        kern,
        grid_spec=pltpu.PrefetchScalarGridSpec(
            num_scalar_prefetch=3,
            grid=(n_tiles, nblk),
            in_specs=in_specs,
            out_specs=pl.BlockSpec((tm, tn), out_map),
            scratch_shapes=[pltpu.VMEM((n_w, k, tn), F32), pltpu.VMEM((n_w, k, tn), BF16),
                            pltpu.SemaphoreType.DMA((n_w,))],
        ),
        out_shape=jax.ShapeDtypeStruct((rows, n_out), out_dtype),
        compiler_params=_cparams(2),
        name=name,
    )(be, nused, nxt, *args)


def _kpe_kernel(x_ref, cs_ref, o_ref):
    g = x_ref[...] * cs_ref[...]
    r = g + pltpu.roll(g, QK_ROPE, axis=1)
    lane = lax.broadcasted_iota(jnp.int32, r.shape, 1)
    o_ref[...] = jnp.where(lane < QK_ROPE, r, 0.0).astype(o_ref.dtype)


def rope_shared_key(proj, cs, col_blk, *, tm=512):
    rows = proj.shape[0]
    tm = min(tm, rows)
    return pl.pallas_call(
        _kpe_kernel,
        grid=(rows // tm,),
        in_specs=[pl.BlockSpec((tm, LANES), lambda i: (i, col_blk)), pl.BlockSpec((tm, LANES), lambda i: (i, 0))],
        out_specs=pl.BlockSpec((tm, LANES), lambda i: (i, 0)),
        out_shape=jax.ShapeDtypeStruct((rows, LANES), BF16),
        compiler_params=_cparams(1),
        name="rope_shared_key",
    )(proj, cs)


def _attn_kernel(q_ref, kv_ref, kpe_ref, o_ref, *, tq, tk, hp):
    qi = pl.program_id(2)
    n_full = (qi * tq) // tk
    n_diag = max(1, tq // tk)
    row = lax.broadcasted_iota(jnp.int32, (tq, tk), 0) + qi * tq
    col0 = lax.broadcasted_iota(jnp.int32, (tq, tk), 1)
    qs = [q_ref[:, a * Q_HEAD_PAD:(a + 1) * Q_HEAD_PAD] for a in range(hp)]

    def step(j, carry, masked):
        r0 = pl.multiple_of(j * tk, tk)
        kpe = kpe_ref[pl.ds(r0, tk), :]
        out = []
        for a in range(hp):
            m, l, acc = carry[a]
            c0 = a * (QK_NOPE + V_HEAD)
            k = jnp.concatenate([kv_ref[pl.ds(r0, tk), c0:c0 + QK_NOPE], kpe], axis=1)
            v = kv_ref[pl.ds(r0, tk), c0 + QK_NOPE:c0 + QK_NOPE + V_HEAD]
            s = lax.dot_general(qs[a], k, (((1,), (1,)), ((), ())), preferred_element_type=F32)
            if masked:
                s = jnp.where(col0 + j * tk <= row, s, NEG_BIG)
            m_new = jnp.maximum(m, jnp.max(s, axis=1, keepdims=True))
            p = jnp.exp2(s - m_new)
            alpha = jnp.exp2(m - m_new)
            l = alpha * l + jnp.sum(p, axis=1, keepdims=True)
            acc = alpha * acc + jnp.dot(p.astype(BF16), v, preferred_element_type=F32)
            out.append((m_new, l, acc))
        return tuple(out)

    init = tuple((jnp.full((tq, 1), NEG_BIG, F32), jnp.zeros((tq, 1), F32), jnp.zeros((tq, V_HEAD), F32))
                 for _ in range(hp))
    carry = lax.fori_loop(0, n_full, lambda j, c: step(j, c, False), init)
    for jd in range(n_diag):
        carry = step(n_full + jd, carry, True)
    for a in range(hp):
        _, l, acc = carry[a]
        o_ref[:, a * V_HEAD:(a + 1) * V_HEAD] = (acc / l).astype(o_ref.dtype)


def mla_attention_core(q, kv, kpe, *, batch, seq, heads, tq=512, tk=512, hp=4):
    tq = min(tq, seq)
    tk = min(tk, seq)
    hp = min(hp, heads)
    assert (tk % tq == 0 or tq % tk == 0) and seq % tk == 0
    nq = seq // tq
    kern = functools.partial(_attn_kernel, tq=tq, tk=tk, hp=hp)
    return pl.pallas_call(
        kern,
        grid=(batch, heads // hp, nq),
        in_specs=[
            pl.BlockSpec((tq, hp * Q_HEAD_PAD), lambda b, h, i: (b * nq + i, h)),
            pl.BlockSpec((seq, hp * (QK_NOPE + V_HEAD)), lambda b, h, i: (b, h)),
            pl.BlockSpec((seq, LANES), lambda b, h, i: (b, 0)),
        ],
        out_specs=pl.BlockSpec((tq, hp * V_HEAD), lambda b, h, i: (b * nq + i, h)),
        out_shape=jax.ShapeDtypeStruct((batch * seq, heads * V_HEAD), BF16),
        compiler_params=_cparams(3),
        name="mla_flash_attention",
    )(q, kv, kpe)


def _gdn_gate_kernel(x_ref, w_ref, alog_ref, dtb_ref, o_ref, *, heads):
    ba = jnp.dot(x_ref[...], w_ref[...].astype(BF16), preferred_element_type=F32)
    lane = lax.broadcasted_iota(jnp.int32, ba.shape, 1)
    beta = jax.nn.sigmoid(ba)
    z = ba + dtb_ref[...]
    softplus = jnp.maximum(z, 0.0) + jnp.log1p(jnp.exp(-jnp.abs(z)))
    g = -jnp.exp(alog_ref[...]) * softplus
    o_ref[...] = jnp.where(lane < heads, beta, g)


def gdn_gates(xn, w_ba, a_log, dt_bias, *, heads, tm=512):
    rows, k = xn.shape
    tm = min(tm, rows)
    pad = LANES - 2 * heads
    w = jnp.pad(w_ba, ((0, 0), (0, pad)))
    alog = jnp.pad(a_log.astype(F32), (heads, pad)).reshape(1, LANES)
    dtb = jnp.pad(dt_bias.astype(F32), (heads, pad)).reshape(1, LANES)
    return pl.pallas_call(
        functools.partial(_gdn_gate_kernel, heads=heads),
        grid=(rows // tm,),
        in_specs=[pl.BlockSpec((tm, k), lambda i: (i, 0)), pl.BlockSpec((k, LANES), lambda i: (0, 0)),
                  pl.BlockSpec((1, LANES), lambda i: (0, 0)), pl.BlockSpec((1, LANES), lambda i: (0, 0))],
        out_specs=pl.BlockSpec((tm, LANES), lambda i: (i, 0)),
        out_shape=jax.ShapeDtypeStruct((rows, LANES), F32),
        compiler_params=_cparams(1),
        name="gdn_gates",
    )(xn, w, alog, dtb)


def _bmm(a, b):
    return lax.dot_general(a, b, (((2,), (1,)), ((0,), (0,))), preferred_element_type=F32)


def _bmm_nt(a, b):
    return lax.dot_general(a, b, (((2,), (2,)), ((0,), (0,))), preferred_element_type=F32)


def _bmm_tn(a, b):
    return lax.dot_general(a, b, (((1,), (1,)), ((0,), (0,))), preferred_element_type=F32)


def _gdn_kernel(q_ref, k_ref, v_ref, z_ref, cwq_ref, cwk_ref, cwv_ref, gate_ref, nw_ref, o_ref,
                halo, xx, lhs_s, bt_s, au_s, egl_s, st, *, hg, ts, heads):
    nc = ts // CHUNK
    cb = min(16, nc)
    hgi = pl.program_id(1)
    first = pl.program_id(2) == 0

    lane_t = lax.broadcasted_iota(jnp.int32, (ts, LANES), 1)
    ii = lax.broadcasted_iota(jnp.int32, (CHUNK, CHUNK), 0)
    jj = lax.broadcasted_iota(jnp.int32, (CHUNK, CHUNK), 1)
    incl = ii >= jj
    strict = ii > jj
    lower_ones = jnp.where(incl, 1.0, 0.0).astype(F32)
    eye = jnp.where(ii == jj, 1.0, 0.0).astype(F32)
    gates = gate_ref[...]

    def conv_silu(idx, x_ref, w_ref, c0):
        x = x_ref[:, c0:c0 + LANES]
        w = w_ref[:, c0:c0 + LANES]
        xb = xx.at[idx * hg + c0 // LANES]
        xb[0:8, :] = jnp.where(first, 0.0, halo[idx, :, c0:c0 + LANES])
        xb[8:ts + 8, :] = x
        halo[idx, :, c0:c0 + LANES] = x[ts - 8:ts, :]
        y = x * w[3:4, :]
        for s in (1, 2, 3):
            y = y + xb[8 - s:8 - s + ts, :] * w[3 - s:4 - s, :]
        return _silu(y)

    @pl.when(first)
    def _():
        st[...] = jnp.zeros(st.shape, F32)

    for hh in range(hg):
        c0 = hh * LANES
        head = hgi * hg + hh
        q = conv_silu(0, q_ref, cwq_ref, c0)
        q = q * lax.rsqrt(jnp.sum(q * q, axis=-1, keepdims=True) + RMS_EPS) * (GDN_DK ** -0.5)
        k = conv_silu(1, k_ref, cwk_ref, c0)
        k = k * lax.rsqrt(jnp.sum(k * k, axis=-1, keepdims=True) + RMS_EPS)
        v = conv_silu(2, v_ref, cwv_ref, c0)

        b_col = jnp.sum(jnp.where(lane_t == head, gates, 0.0), axis=1, keepdims=True)
        g_col = jnp.sum(jnp.where(lane_t == heads + head, gates, 0.0), axis=1, keepdims=True)
        for sb in range(nc // cb):
            r0, r1 = sb * cb * CHUNK, (sb + 1) * cb * CHUNK
            cs0, cs1 = sb * cb, (sb + 1) * cb
            b3 = b_col[r0:r1].reshape(cb, CHUNK, 1)
            g3 = g_col[r0:r1].reshape(cb, CHUNK, 1)
            g_row3 = jnp.sum(g3 * eye, axis=1, keepdims=True)
            gc3 = jnp.sum(lower_ones * g_row3, axis=2, keepdims=True)
            gc_row3 = jnp.sum(gc3 * eye, axis=1, keepdims=True)
            gl3 = gc3[:, CHUNK - 1:CHUNK, :]
            decay = jnp.where(incl, jnp.exp(jnp.where(incl, gc3 - gc_row3, 0.0)), 0.0)
            egc3 = jnp.exp(gc3)

            q3 = q[r0:r1].reshape(cb, CHUNK, LANES)
            k3 = k[r0:r1].reshape(cb, CHUNK, LANES)
            v3 = v[r0:r1].reshape(cb, CHUNK, LANES)
            q16 = q3.astype(BF16)
            k16 = k3.astype(BF16)
            lower = jnp.where(strict, _bmm_nt(k16, k16) * decay * b3, 0.0)
            p = -lower
            t_inv = eye + p
            for _ in range(5):
                p16 = p.astype(BF16)
                p = _bmm(p16, p16)
                t_inv = t_inv + _bmm(t_inv.astype(BF16), p.astype(BF16))
            vk16 = jnp.concatenate([(v3 * b3).astype(BF16), (k3 * (b3 * egc3)).astype(BF16)], axis=-1)
            uw16 = _bmm(t_inv.astype(BF16), vk16).astype(BF16)
            a16 = jnp.where(incl, _bmm_nt(q16, k16) * decay, 0.0).astype(BF16)
            awu = _bmm(a16, uw16)
            kd16 = (k3 * jnp.exp(gl3 - gc3)).astype(BF16)
            mb = _bmm_tn(kd16, uw16)
            lhs_s[hh, cs0:cs1, 0:GDN_DK, :] = mb[:, :, LANES:].astype(BF16)
            lhs_s[hh, cs0:cs1, GDN_DK:GDN_DK + CHUNK, :] = (q3 * egc3 - awu[:, :, LANES:]).astype(BF16)
            bt_s[hh, cs0:cs1] = mb[:, :, 0:LANES]
            au_s[hh, cs0:cs1] = awu[:, :, 0:LANES]
            egl_s[hh, cs0:cs1] = jnp.broadcast_to(jnp.exp(gl3), (cb, 1, LANES))

    def chunk_body(c, carry):
        s_old = [st[hh] for hh in range(hg)]
        res = [jnp.dot(lhs_s[hh, c], s_old[hh].astype(BF16), preferred_element_type=F32) for hh in range(hg)]
        for hh in range(hg):
            st[hh] = egl_s[hh, c] * s_old[hh] - res[hh][0:GDN_DK, :] + bt_s[hh, c]
            au_s[hh, c] = res[hh][GDN_DK:GDN_DK + CHUNK, :] + au_s[hh, c]
        return carry

    lax.fori_loop(0, nc, chunk_body, 0)

    nw = nw_ref[...]
    for hh in range(hg):
        c0 = hh * LANES
        o = au_s[hh].reshape(ts, LANES)
        on = o * lax.rsqrt(jnp.mean(o * o, axis=-1, keepdims=True) + RMS_EPS) * nw
        o_ref[:, c0:c0 + LANES] = (on * _silu(z_ref[:, c0:c0 + LANES])).astype(o_ref.dtype)


def gdn_delta(proj, conv_w, gates, norm_w, *, batch, seq, heads, hg=4, ts=1024):
    hg = min(hg, heads)
    ts = min(ts, seq)
    nhb = heads // hg
    nst = seq // ts
    wblk = hg * LANES
    nc = ts // CHUNK
    kern = functools.partial(_gdn_kernel, hg=hg, ts=ts, heads=heads)

    def pmap(off):
        return lambda b, h, i: (b * nst + i, off * nhb + h)

    def cmap(off):
        return lambda b, h, i: (0, off * nhb + h)

    return pl.pallas_call(
        kern,
        grid=(batch, nhb, nst),
        in_specs=[pl.BlockSpec((ts, wblk), pmap(0)), pl.BlockSpec((ts, wblk), pmap(1)),
                  pl.BlockSpec((ts, wblk), pmap(2)), pl.BlockSpec((ts, wblk), pmap(3)),
                  pl.BlockSpec((4, wblk), cmap(0)), pl.BlockSpec((4, wblk), cmap(1)), pl.BlockSpec((4, wblk), cmap(2)),
                  pl.BlockSpec((ts, LANES), lambda b, h, i: (b * nst + i, 0)),
                  pl.BlockSpec((1, LANES), lambda b, h, i: (0, 0))],
        out_specs=pl.BlockSpec((ts, wblk), lambda b, h, i: (b * nst + i, h)),
        out_shape=jax.ShapeDtypeStruct((batch * seq, heads * GDN_DV), BF16),
        scratch_shapes=[pltpu.VMEM((3, 8, wblk), F32),
                        pltpu.VMEM((3 * hg, ts + 8, LANES), F32),
                        pltpu.VMEM((hg, nc, GDN_DK + CHUNK, LANES), BF16),
                        pltpu.VMEM((hg, nc, GDN_DK, GDN_DV), F32),
                        pltpu.VMEM((hg, nc, CHUNK, GDN_DV), F32),
                        pltpu.VMEM((hg, nc, 1, LANES), F32),
                        pltpu.VMEM((hg, GDN_DK, GDN_DV), F32)],
        compiler_params=_cparams(3),
        name="gdn_delta_rule",
    )(proj, proj, proj, proj, conv_w, conv_w, conv_w, gates, norm_w.reshape(1, LANES).astype(F32))


def _router_kernel(x_ref, g_ref, w_ref, b_ref, o_ref, pk_ref):
    xf = x_ref[...]
    var = jnp.mean(xf * xf, axis=-1, keepdims=True)
    xn = xf * lax.rsqrt(var + RMS_EPS) * g_ref[...]
    w = w_ref[...]
    x_hi = xn.astype(BF16)
    x_lo = (xn - x_hi.astype(F32)).astype(BF16)
    w_hi = w.astype(BF16)
    w_lo = (w - w_hi.astype(F32)).astype(BF16)
    acc = jnp.dot(x_hi, w_hi, preferred_element_type=F32)
    acc = acc + jnp.dot(x_lo, w_hi, preferred_element_type=F32) + jnp.dot(x_hi, w_lo, preferred_element_type=F32)
    o_ref[...] = acc + b_ref[...]
    half = xn.shape[1] // 2
    bits = pltpu.bitcast(x_hi.astype(F32), jnp.uint32)
    pk_ref[...] = bits[:, half:] | (bits[:, :half] >> 16)


def router_logits(h, gain, router_w, router_b, *, tm=512):
    rows, d = h.shape
    tm = min(tm, rows)
    n_e = router_w.shape[1]
    w = jnp.pad(router_w, ((0, 0), (0, LANES - n_e)))
    b = jnp.pad(router_b.astype(F32), (0, LANES - n_e)).reshape(1, LANES)
    return pl.pallas_call(
        _router_kernel,
        grid=(rows // tm,),
        in_specs=[pl.BlockSpec((tm, d), lambda i: (i, 0)), pl.BlockSpec((1, d), lambda i: (0, 0)),
                  pl.BlockSpec((d, LANES), lambda i: (0, 0)), pl.BlockSpec((1, LANES), lambda i: (0, 0))],
        out_specs=[pl.BlockSpec((tm, LANES), lambda i: (i, 0)), pl.BlockSpec((tm, d // 2), lambda i: (i, 0))],
        out_shape=[jax.ShapeDtypeStruct((rows, LANES), F32), jax.ShapeDtypeStruct((rows, d // 2), jnp.uint32)],
        compiler_params=_cparams(1),
        name="moe_router",
    )(h, gain.reshape(1, d).astype(F32), w, b)


def _row_copy(src_hbm, dst, sem, src_row, dst_row):
    return pltpu.make_async_copy(src_hbm.at[pl.ds(src_row, 1)], dst.at[pl.ds(dst_row, 1)], sem)


def _gather_unpack_kernel(tok_ref, cnt_ref, xp_hbm, o_ref, buf, sem, *, tm, nblk):
    rb = pl.program_id(0)

    def n_groups(blk):
        return (cnt_ref[blk] + ROWS_PER_ISSUE - 1) // ROWS_PER_ISSUE

    def issue(blk, slot):
        def group(g, carry):
            for u in range(ROWS_PER_ISSUE):
                r = g * ROWS_PER_ISSUE + u
                _row_copy(xp_hbm, buf.at[slot], sem.at[slot], tok_ref[blk * tm + r], r).start(priority=u % 2)
            return carry
        lax.fori_loop(0, n_groups(blk), group, 0)

    @pl.when(rb == 0)
    def _():
        buf[...] = jnp.zeros(buf.shape, buf.dtype)
        issue(0, 0)

    @pl.when(rb + 1 < nblk)
    def _():
        issue(rb + 1, (rb + 1) % 2)

    slot = rb % 2
    n_rows = pl.multiple_of(n_groups(rb) * ROWS_PER_ISSUE, ROWS_PER_ISSUE)

    @pl.when(n_rows > 0)
    def _():
        pltpu.make_async_copy(xp_hbm.at[pl.ds(0, n_rows)], buf.at[slot, pl.ds(0, n_rows)], sem.at[slot]).wait()

    w = buf[slot]
    half = w.shape[1]
    o_ref[:, 0:half] = pltpu.bitcast(w << 16, F32).astype(o_ref.dtype)
    o_ref[:, half:2 * half] = pltpu.bitcast(w & jnp.uint32(0xFFFF0000), F32).astype(o_ref.dtype)


def gather_rows(xp, slot_tok, blk_cnt, *, tm):
    rows, half = xp.shape
    n_slots = slot_tok.shape[0]
    nblk = n_slots // tm
    kern = functools.partial(_gather_unpack_kernel, tm=tm, nblk=nblk)
    return pl.pallas_call(
        kern,
        grid_spec=pltpu.PrefetchScalarGridSpec(
            num_scalar_prefetch=2,
            grid=(nblk,),
            in_specs=[pl.BlockSpec(memory_space=pl.ANY)],
            out_specs=pl.BlockSpec((tm, 2 * half), lambda i, t, c: (i, 0)),
            scratch_shapes=[pltpu.VMEM((2, tm, half), jnp.uint32), pltpu.SemaphoreType.DMA((2,))],
        ),
        out_shape=jax.ShapeDtypeStruct((n_slots, 2 * half), BF16),
        compiler_params=_cparams(1),
        name="moe_gather_rows",
    )(slot_tok, blk_cnt, xp)


def _combine_kernel(slot_ref, y_hbm, h_ref, gate_ref, g_ref, o_ref, buf, sem, *, tc, nblk):
    i = pl.program_id(0)

    def issue(blk, slot):
        def group(g, carry):
            for u in range(ROWS_PER_ISSUE):
                r = g * ROWS_PER_ISSUE + u
                for kk in range(TOP_K):
                    src = slot_ref[(blk * tc + r) * TOP_K + kk]
                    _row_copy(y_hbm, buf.at[slot, kk], sem.at[slot], src, r).start(priority=kk % 2)
            return carry
        lax.fori_loop(0, tc // ROWS_PER_ISSUE, group, 0)

    @pl.when(i == 0)
    def _():
        issue(0, 0)

    @pl.when(i + 1 < nblk)
    def _():
        issue(i + 1, (i + 1) % 2)

    slot = i % 2
    for kk in range(TOP_K):
        pltpu.make_async_copy(y_hbm.at[pl.ds(0, tc)], buf.at[slot, kk], sem.at[slot]).wait()
    gate = gate_ref[...]
    xf = h_ref[...] + gate[:, 0:1] * buf[slot, 0] + gate[:, 1:2] * buf[slot, 1]
    var = jnp.mean(xf * xf, axis=-1, keepdims=True)
    o_ref[...] = xf * lax.rsqrt(var + RMS_EPS) * g_ref[...]


def combine_norm(h, y, tok_slots, gates, final_gain, *, tc=256):
    rows, d = h.shape
    tc = min(tc, rows)
    nblk = rows // tc
    kern = functools.partial(_combine_kernel, tc=tc, nblk=nblk)
    return pl.pallas_call(
        kern,
        grid_spec=pltpu.PrefetchScalarGridSpec(
            num_scalar_prefetch=1,
            grid=(nblk,),
            in_specs=[pl.BlockSpec(memory_space=pl.ANY), pl.BlockSpec((tc, d), lambda i, s: (i, 0)),
                      pl.BlockSpec((tc, TOP_K), lambda i, s: (i, 0)), pl.BlockSpec((1, d), lambda i, s: (0, 0))],
            out_specs=pl.BlockSpec((tc, d), lambda i, s: (i, 0)),
            scratch_shapes=[pltpu.VMEM((2, TOP_K, tc, d), F32), pltpu.SemaphoreType.DMA((2,))],
        ),
        out_shape=jax.ShapeDtypeStruct((rows, d), F32),
        compiler_params=_cparams(1),
        name="moe_combine_norm",
    )(tok_slots, y, h, gates, final_gain.reshape(1, d).astype(F32))


def _route(logits, n_experts, tm):
    n_tok = logits.shape[0]
    n_asg = n_tok * TOP_K
    nblk = n_asg // tm + n_experts
    top_logit, top_idx = lax.top_k(logits[:, :n_experts], TOP_K)
    gates = jax.nn.softmax(top_logit, axis=-1)
    flat_e = top_idx.reshape(n_asg).astype(jnp.int32)
    onehot = (flat_e[:, None] == jnp.arange(n_experts, dtype=jnp.int32)[None, :]).astype(jnp.int32)
    csum = jnp.cumsum(onehot, axis=0)
    rank = jnp.sum(csum * onehot, axis=1) - 1
    counts = csum[-1]
    padded = (counts + tm - 1) // tm * tm
    pad_end = jnp.cumsum(padded)
    pad_start = pad_end - padded
    dest = (pad_start[flat_e] + rank).astype(jnp.int32)
    slot_tok = jnp.zeros((nblk * tm,), jnp.int32).at[dest].set(jnp.arange(n_asg, dtype=jnp.int32) // TOP_K)
    nused = (pad_end[-1] // tm).astype(jnp.int32)
    blk = jnp.arange(nblk, dtype=jnp.int32)
    be = jnp.minimum(jnp.searchsorted(pad_end, blk * tm, side="right"), n_experts - 1).astype(jnp.int32)
    be = jnp.where(blk < nused, be, be[jnp.maximum(nused - 1, 0)])
    seg_end = pad_end[be] // tm
    nxt = jnp.where(seg_end < nused, be[jnp.minimum(seg_end, nblk - 1)], -1).astype(jnp.int32)
    blk_cnt = jnp.where(blk < nused, jnp.clip(counts[be] - (blk * tm - pad_start[be]), 0, tm), 0).astype(jnp.int32)
    return slot_tok, gates, dest, be, nused.reshape(1), nxt, blk_cnt


def _rope_table(positions):
    inv_freq = ROPE_THETA ** (-jnp.arange(0, QK_ROPE, 2, dtype=F32) / QK_ROPE)
    ang = positions.astype(F32)[..., None] * inv_freq
    cos, sin = jnp.cos(ang), jnp.sin(ang)
    return jnp.concatenate([cos, cos, sin, sin], axis=-1).reshape(-1, 2 * QK_ROPE)


def _rot_cols(w):
    half = w.shape[-1] // 2
    return jnp.concatenate([-w[..., half:], w[..., :half]], axis=-1)


def _mla_layer(h, cs, ln, w_in, q_norm, w_qb, kv_norm, w_kvb, w_o, *, batch, seq):
    d = h.shape[1]
    q_lora = q_norm.shape[0]
    kv_lora = kv_norm.shape[0]
    heads = w_o.shape[0] // V_HEAD
    assert q_lora == kv_lora and q_lora % LANES == 0
    w_rope = w_in[:, q_lora + kv_lora:]
    w_in_p = jnp.concatenate([w_in, _rot_cols(w_rope)], axis=1)
    n_in = w_in_p.shape[1]
    wq = w_qb.reshape(q_lora, heads, QK_NOPE + QK_ROPE)
    wq_p = jnp.concatenate([wq, _rot_cols(wq[..., QK_NOPE:])], axis=-1).reshape(q_lora, heads * Q_HEAD_PAD)

    proj = gmm(h, [w_in_p[None]], n_out=n_in, tm=512, tn=n_in, out_dtype=F32, gain=ln, name="mla_in_proj")
    scale = LOG2_E * (QK_NOPE + QK_ROPE) ** -0.5
    q = gmm(proj, [wq_p[None]], n_out=heads * Q_HEAD_PAD, tm=2048, tn=1024, out_dtype=BF16, mode="rope", cs=cs,
            scale=scale, gain=q_norm, x_col_blk=0, name="mla_q_proj")
    kv = gmm(proj, [w_kvb[None]], n_out=heads * (QK_NOPE + V_HEAD), tm=2048, tn=1024, out_dtype=BF16,
             gain=kv_norm, x_col_blk=1, name="mla_kv_proj")
    kpe = rope_shared_key(proj, cs, (q_lora + kv_lora) // LANES)
    o = mla_attention_core(q, kv, kpe, batch=batch, seq=seq, heads=heads)
    return gmm(o, [w_o[None]], n_out=d, tm=1024, tn=1024, out_dtype=F32, residual=h, name="mla_out_proj")


def _ffn_layer(h, ln, w_gate, w_up, w_down):
    d = h.shape[1]
    f = w_gate.shape[1]
    hn = rmsnorm(h, ln)
    mid = gmm(hn, [w_gate[None], w_up[None]], n_out=f, tm=1024, tn=1024 if f % 1024 == 0 else f // 7,
              out_dtype=BF16, mode="swiglu", name="ffn_up")
    return gmm(mid, [w_down[None]], n_out=d, tm=512, tn=512, out_dtype=F32, residual=h, name="ffn_down")


def _gdn_layer(h, ln, w_in, conv_w, a_log, dt_bias, norm_w, w_o, *, batch, seq):
    d = h.shape[1]
    heads = a_log.shape[0]
    n_main = 4 * heads * LANES
    hn = rmsnorm(h, ln)
    proj = gmm(hn, [w_in[None]], n_out=n_main, tm=1024, tn=1024, out_dtype=F32, name="gdn_in_proj")
    gates = gdn_gates(hn, w_in[:, n_main:], a_log, dt_bias, heads=heads)
    o = gdn_delta(proj, conv_w, gates, norm_w, batch=batch, seq=seq, heads=heads)
    return gmm(o, [w_o[None]], n_out=d, tm=1024, tn=1024, out_dtype=F32, residual=h, name="gdn_out_proj")


def _moe_layer_final(h, ln, router_w, router_b, w_gate, w_up, w_down, final_gain, *, tm=512):
    rows, d = h.shape
    n_e, _, f = w_gate.shape
    tm = min(tm, rows // 4)
    logits, xn_packed = router_logits(h, ln, router_w, router_b)
    slot_tok, gates, dest, be, nused, nxt, blk_cnt = _route(logits, n_e, tm)
    xs = gather_rows(xn_packed, slot_tok, blk_cnt, tm=tm)
    mid = gmm(xs, [w_gate, w_up], n_out=f, tm=tm, tn=1024 if f % 1024 == 0 else f // 7, out_dtype=BF16,
              be=be, nused=nused, nxt=nxt, mode="swiglu", name="moe_up")
    y = gmm(mid, [w_down], n_out=d, tm=tm, tn=512, out_dtype=F32, be=be, nused=nused, nxt=nxt, name="moe_down")
    return combine_norm(h, y, dest, gates, final_gain)


def kernel(x, positions, ln_mix_mla, mla_w_in, mla_q_norm, mla_w_qb, mla_kv_norm, mla_w_kvb, mla_w_o, ln_ffn_dense, ffn_w_gate, ffn_w_up, ffn_w_down, ln_mix_gdn, gdn_w_in, gdn_conv_w, gdn_a_log, gdn_dt_bias, gdn_norm, gdn_w_o, ln_ffn_moe, moe_router, moe_router_bias, moe_w_gate, moe_w_up, moe_w_down, final_norm):
    batch, seq, d = x.shape
    assert ln_mix_mla.shape[0] == 1 and ln_mix_gdn.shape[0] == 1, "two-layer trunk: one MLA and one DeltaNet layer"
    h = x.reshape(batch * seq, d)
    cs = _rope_table(positions)
    h = _mla_layer(h, cs, ln_mix_mla[0], mla_w_in[0], mla_q_norm[0], mla_w_qb[0], mla_kv_norm[0], mla_w_kvb[0],
                   mla_w_o[0], batch=batch, seq=seq)
    h = _ffn_layer(h, ln_ffn_dense[0], ffn_w_gate[0], ffn_w_up[0], ffn_w_down[0])
    h = _gdn_layer(h, ln_mix_gdn[0], gdn_w_in[0], gdn_conv_w[0], gdn_a_log[0], gdn_dt_bias[0], gdn_norm[0],
                   gdn_w_o[0], batch=batch, seq=seq)
    out = _moe_layer_final(h, ln_ffn_moe[0], moe_router[0], moe_router_bias[0], moe_w_gate[0], moe_w_up[0],
                           moe_w_down[0], final_norm)
    return out.reshape(batch, seq, d)
```
